```python
import jax, jax.numpy as jnp
from jax import lax
import numpy as np

D_MODEL = 1024
BATCH = 8
SEQ = 4096
DEPTH = 2

ATTN_HEAD_DIM = 64
ATTN_HEADS_PER_GROUP = 4
ATTN_GROUPS = ((128, 1), (512, 4), (2048, 16))
N_ATTN_GROUPS = 3
ATTN_HEADS = ATTN_HEADS_PER_GROUP * N_ATTN_GROUPS
ATTN_WIDTH = ATTN_HEADS * ATTN_HEAD_DIM
ATTN_OUT_WIDTH = ATTN_HEADS_PER_GROUP * ATTN_HEAD_DIM
ATTN_BLOCK = 128

DN_HEADS = 4
DN_HEAD_DIM = 128
DN_WIDTH = DN_HEADS * DN_HEAD_DIM
DN_CONV = 4
DN_CHUNK = 64

D_FF = 2816
FFN_CONV = 3

NORM_EPS = 1e-6

IN_SIZES = (ATTN_WIDTH, ATTN_WIDTH, ATTN_WIDTH,
            3 * DN_WIDTH,
            DN_HEADS, DN_HEADS,
            DN_WIDTH,
            D_MODEL, D_MODEL)
IN_WIDTH = sum(IN_SIZES)

kernel_name = "hybrid_dilated_attn_gated_deltanet_convglu"


def rms_norm(x, w):
    x32 = x.astype(jnp.float32)
    y = x32 * lax.rsqrt(jnp.mean(x32 * x32, axis=-1, keepdims=True) + NORM_EPS)
    return (y * w.astype(jnp.float32)).astype(x.dtype)


def l2_norm(x):
    return x * lax.rsqrt(jnp.sum(x * x, axis=-1, keepdims=True) + NORM_EPS)


def causal_dwconv(x, w):
    K = w.shape[0]
    S = x.shape[1]
    xp = jnp.pad(x, ((0, 0), (K - 1, 0), (0, 0)))
    return sum(xp[:, j:j + S] * w[j] for j in range(K))


def dilated_window_attention(q, k, v, window, dilation):
    B, S, H, Dh = q.shape
    span = window // dilation
    L = S // dilation
    nb = -(-L // ATTN_BLOCK)
    Lp = nb * ATTN_BLOCK
    Bp = B * dilation

    def to_blocks(t):
        t = t.astype(jnp.float32).reshape(B, L, dilation, H, Dh).transpose(0, 2, 1, 3, 4)
        t = t.reshape(Bp, L, H, Dh)
        t = jnp.pad(t, ((0, 0), (0, Lp - L), (0, 0), (0, 0)))
        return t.reshape(Bp, nb, ATTN_BLOCK, H, Dh)

    qb, kb, vb = to_blocks(q), to_blocks(k), to_blocks(v)

    def band_keys(t):
        prev = jnp.pad(t[:, :-1], ((0, 0), (1, 0), (0, 0), (0, 0), (0, 0)))
        return jnp.concatenate([prev, t], axis=2)

    kw, vw = band_keys(kb), band_keys(vb)
    s = jnp.einsum('bnqhd,bnkhd->bnhqk', qb, kw) * (Dh ** -0.5)
    blk = jnp.arange(nb)[:, None] * ATTN_BLOCK
    qpos = blk + jnp.arange(ATTN_BLOCK)[None, :]
    kpos = blk - ATTN_BLOCK + jnp.arange(2 * ATTN_BLOCK)[None, :]
    rel = qpos[:, :, None] - kpos[:, None, :]
    valid = (rel >= 0) & (rel <= span) & (kpos[:, None, :] >= 0)
    s = jnp.where(valid[None, :, None], s, -jnp.inf)
    m = jnp.max(s, axis=-1, keepdims=True)
    p = jnp.exp(s - m)
    den = jnp.sum(p, axis=-1)
    o = jnp.einsum('bnhqk,bnkhd->bnqhd', p, vw) / jnp.swapaxes(den, 2, 3)[..., None]
    lse = jnp.swapaxes(m[..., 0] + jnp.log(den), 2, 3)

    o = o.reshape(Bp, Lp, H, Dh)[:, :L].reshape(B, dilation, L, H, Dh)
    o = o.transpose(0, 2, 1, 3, 4).reshape(B, S, H, Dh)
    lse = lse.reshape(Bp, Lp, H)[:, :L].reshape(B, dilation, L, H)
    lse = lse.transpose(0, 2, 1, 3).reshape(B, S, H)
    return o, lse


def gated_delta_rule_chunked(q, k, v, g, beta):
    B, S, H, dk = q.shape
    dv = v.shape[-1]
    C = DN_CHUNK
    N = S // C

    def chunk(t):
        return t.reshape(B, N, C, H, -1).transpose(0, 3, 1, 2, 4)

    qc, kc, vc = chunk(q), chunk(k), chunk(v)
    gc = jnp.cumsum(g.reshape(B, N, C, H).transpose(0, 3, 1, 2), axis=-1)
    bc = beta.reshape(B, N, C, H).transpose(0, 3, 1, 2)[..., None]
    kb, vb = kc * bc, vc * bc

    incl = jnp.tril(jnp.ones((C, C), dtype=bool))
    strict = jnp.tril(jnp.ones((C, C), dtype=bool), -1)
    decay = jnp.exp(jnp.where(incl, gc[..., :, None] - gc[..., None, :], -jnp.inf))

    M = jnp.where(strict, jnp.einsum('bhncd,bhnjd->bhncj', kb, kc) * decay, 0.0)
    A = jnp.eye(C, dtype=jnp.float32) + M
    rhs = jnp.concatenate([vb, kb * jnp.exp(gc)[..., None]], axis=-1)
    uw = lax.linalg.triangular_solve(A, rhs, left_side=True, lower=True, unit_diagonal=True)
    u, w = uw[..., :dv], uw[..., dv:]

    qk = jnp.einsum('bhncd,bhnjd->bhncj', qc, kc) * decay
    q_dec = qc * jnp.exp(gc)[..., None]
    k_dec = kc * jnp.exp(gc[..., -1:] - gc)[..., None]
    g_last = jnp.exp(gc[..., -1])

    def step(state, inp):
        u_i, w_i, qk_i, qd_i, kd_i, gl_i = inp
        v_new = u_i - jnp.einsum('bhcd,bhde->bhce', w_i, state)
        o_i = (jnp.einsum('bhcd,bhde->bhce', qd_i, state)
               + jnp.einsum('bhcj,bhje->bhce', qk_i, v_new))
        state = state * gl_i[..., None, None] + jnp.einsum('bhcd,bhce->bhde', kd_i, v_new)
        return state, o_i

    xs = tuple(jnp.moveaxis(t, 2, 0) for t in (u, w, qk, q_dec, k_dec, g_last))
    state0 = jnp.zeros((B, H, dk, dv), dtype=jnp.float32)
    _, o = lax.scan(step, state0, xs)
    return o.transpose(1, 0, 3, 2, 4).reshape(B, S, H, dv)


def hybrid_mixer(xn, w_in, dn_conv_w, dn_a_log, dn_dt_bias, dn_onorm_w, w_pa, w_pb, w_o):
    B, S, _ = xn.shape
    proj = xn @ w_in
    split_at = [int(i) for i in np.cumsum(IN_SIZES)[:-1]]
    aq, ak, av, dqkv, dbeta, da, dz, gate_a, gate_b = jnp.split(proj, split_at, axis=-1)

    hs = (B, S, N_ATTN_GROUPS, ATTN_HEADS_PER_GROUP, ATTN_HEAD_DIM)
    aq, ak, av = aq.reshape(hs), ak.reshape(hs), av.reshape(hs)
    outs, lses = [], []
    for gi, (window, dilation) in enumerate(ATTN_GROUPS):
        o_g, lse_g = dilated_window_attention(aq[:, :, gi], ak[:, :, gi], av[:, :, gi], window, dilation)
        outs.append(o_g)
        lses.append(lse_g)
    alpha = jax.nn.softmax(jnp.stack(lses, axis=0), axis=0)
    o_a = jnp.sum(alpha[..., None] * jnp.stack(outs, axis=0), axis=0)
    o_a = o_a.reshape(B, S, ATTN_OUT_WIDTH).astype(xn.dtype)

    dqkv = jax.nn.silu(causal_dwconv(dqkv, dn_conv_w)).astype(jnp.float32)
    dq, dk, dv = jnp.split(dqkv, 3, axis=-1)
    hd = (B, S, DN_HEADS, DN_HEAD_DIM)
    dq = l2_norm(dq.reshape(hd)) * (DN_HEAD_DIM ** -0.5)
    dk = l2_norm(dk.reshape(hd))
    dv = dv.reshape(hd)
    beta = jax.nn.sigmoid(dbeta.astype(jnp.float32))
    g = -jnp.exp(dn_a_log.astype(jnp.float32)) * jax.nn.softplus(
        da.astype(jnp.float32) + dn_dt_bias.astype(jnp.float32))
    o_b = gated_delta_rule_chunked(dq, dk, dv, g, beta)
    o_b = rms_norm(o_b, dn_onorm_w) * jax.nn.silu(dz.astype(jnp.float32).reshape(hd))
    o_b = o_b.reshape(B, S, DN_WIDTH).astype(xn.dtype)

    y = jax.nn.sigmoid(gate_a) * (o_a @ w_pa) + jax.nn.sigmoid(gate_b) * (o_b @ w_pb)
    return y @ w_o


def conv_glu_ffn(xn, w_up, conv_w, conv_b, w_down):
    h = causal_dwconv(xn @ w_up, conv_w) + conv_b
    gate, val = jnp.split(h, 2, axis=-1)
    return (jax.nn.silu(gate) * val) @ w_down


def setup_inputs(seed: int = 0) -> dict:
    key = jax.random.key(seed)
    ks = jax.random.split(key, 20)
    f32 = jnp.float32

    def nrm(k, shape, scale):
        return jax.random.normal(k, shape, f32) * scale

    dt = jnp.exp(jax.random.uniform(ks[5], (DEPTH, DN_HEADS), f32)
                 * (jnp.log(0.1) - jnp.log(0.001)) + jnp.log(0.001))
    return {
        "x": nrm(ks[0], (BATCH, SEQ, D_MODEL), 1.0),
        "norm1_w": 1.0 + nrm(ks[1], (DEPTH, D_MODEL), 0.02),
        "w_in": nrm(ks[2], (DEPTH, D_MODEL, IN_WIDTH), D_MODEL ** -0.5),
        "dn_conv_w": nrm(ks[3], (DEPTH, DN_CONV, 3 * DN_WIDTH), DN_CONV ** -0.5),
        "dn_a_log": jnp.log(jax.random.uniform(ks[4], (DEPTH, DN_HEADS), f32, 1.0, 16.0)),
        "dn_dt_bias": dt + jnp.log(-jnp.expm1(-dt)),
        "dn_onorm_w": 1.0 + nrm(ks[6], (DEPTH, DN_HEAD_DIM), 0.02),
        "w_pa": nrm(ks[7], (DEPTH, ATTN_OUT_WIDTH, D_MODEL), ATTN_OUT_WIDTH ** -0.5),
        "w_pb": nrm(ks[8], (DEPTH, DN_WIDTH, D_MODEL), DN_WIDTH ** -0.5),
        "w_o": nrm(ks[9], (DEPTH, D_MODEL, D_MODEL), D_MODEL ** -0.5),
        "norm2_w": 1.0 + nrm(ks[10], (DEPTH, D_MODEL), 0.02),
        "w_up": nrm(ks[11], (DEPTH, D_MODEL, 2 * D_FF), D_MODEL ** -0.5),
        "ffn_conv_w": nrm(ks[12], (DEPTH, FFN_CONV, 2 * D_FF), FFN_CONV ** -0.5),
        "ffn_conv_b": nrm(ks[13], (DEPTH, 2 * D_FF), 0.01),
        "w_down": nrm(ks[14], (DEPTH, D_FF, D_MODEL), D_FF ** -0.5),
        "final_norm_w": 1.0 + nrm(ks[15], (D_MODEL,), 0.02),
    }


def reference(x, norm1_w, w_in, dn_conv_w, dn_a_log, dn_dt_bias, dn_onorm_w, w_pa, w_pb, w_o,
              norm2_w, w_up, ffn_conv_w, ffn_conv_b, w_down, final_norm_w):
    for l in range(DEPTH):
        h = x + hybrid_mixer(rms_norm(x, norm1_w[l]), w_in[l], dn_conv_w[l], dn_a_log[l],
                             dn_dt_bias[l], dn_onorm_w[l], w_pa[l], w_pb[l], w_o[l])
        x = h + conv_glu_ffn(rms_norm(h, norm2_w[l]), w_up[l], ffn_conv_w[l], ffn_conv_b[l], w_down[l])
    return rms_norm(x, final_norm_w)
```

```python
import functools

import jax
import jax.numpy as jnp
from jax import lax
from jax.experimental import pallas as pl
from jax.experimental.pallas import tpu as pltpu

F32 = jnp.float32
BF16 = jnp.bfloat16
HIGHEST = lax.Precision.HIGHEST

NORM_EPS = 1e-6

ATTN_HEAD_DIM = 64
ATTN_HEADS_PER_GROUP = 4
ATTN_GROUP_WIDTH = ATTN_HEADS_PER_GROUP * ATTN_HEAD_DIM
ATTN_GROUPS = ((128, 1), (512, 4), (2048, 16))
ATTN_BLOCK = 128
DN_HEADS = 4
DN_HEAD_DIM = 128
DN_WIDTH = DN_HEADS * DN_HEAD_DIM
DN_CONV = 4
DN_CHUNK = 64
FFN_CONV = 3

LANES = 128
SUBLANES_F32 = 8
SUBLANES_BF16 = 16
VMEM_LIMIT_BYTES = 56 * 1024 * 1024

ROW_TILE = 512


def _resident(shape):
    nd = len(shape)
    return pl.BlockSpec(shape, lambda *_: (0,) * nd, pipeline_mode=pl.Buffered(1))


def _params(*semantics):
    return pltpu.CompilerParams(dimension_semantics=semantics, vmem_limit_bytes=VMEM_LIMIT_BYTES)


def _rms_norm(x, w):
    return x * lax.rsqrt(jnp.mean(x * x, axis=-1, keepdims=True) + NORM_EPS) * w


def _silu(x):
    return x * (1.0 / (1.0 + jnp.exp(-x)))


def _sigmoid(x):
    return 1.0 / (1.0 + jnp.exp(-x))


def _inproj_body(x_ref, nw_ref, w_ref, *out_refs):
    xn = _rms_norm(x_ref[...], nw_ref[...]).astype(BF16)
    c0 = 0
    for o_ref in out_refs:
        n = o_ref.shape[-1]
        o_ref[...] = jnp.dot(xn, w_ref[:, c0:c0 + n], preferred_element_type=F32).astype(o_ref.dtype)
        c0 += n


def _in_projection(x2d, norm_w, w_packed, out_widths, out_dtypes):
    T, D = x2d.shape
    grid = (T // ROW_TILE,)
    row = lambda i: (i, 0)
    return pl.pallas_call(
        _inproj_body,
        grid=grid,
        in_specs=[pl.BlockSpec((ROW_TILE, D), row), _resident((1, D)), _resident(w_packed.shape)],
        out_specs=[pl.BlockSpec((ROW_TILE, n), row) for n in out_widths],
        out_shape=[jax.ShapeDtypeStruct((T, n), dt) for n, dt in zip(out_widths, out_dtypes)],
        compiler_params=_params("parallel"),
        name="in_projection",
    )(x2d, norm_w.reshape(1, D), w_packed)


def _attn_body(q_ref, kp_ref, kc_ref, vp_ref, vc_ref, o_ref, lse_ref, *, span):
    blk = ATTN_BLOCK
    i = pl.program_id(2)
    q = q_ref[...]
    k = jnp.concatenate([kp_ref[...], kc_ref[...]], axis=0)
    v = jnp.concatenate([vp_ref[...], vc_ref[...]], axis=0)
    row = lax.broadcasted_iota(jnp.int32, (blk, 2 * blk), 0)
    col = lax.broadcasted_iota(jnp.int32, (blk, 2 * blk), 1)
    rel = row + blk - col
    first_key = jnp.where(i > 0, 0, blk)
    valid = (rel >= 0) & (rel <= span) & (col >= first_key)
    lane_head = lax.broadcasted_iota(jnp.int32, (blk, ATTN_GROUP_WIDTH), 1) // ATTN_HEAD_DIM
    lse_lane = lax.broadcasted_iota(jnp.int32, (blk, LANES), 1)
    o_acc = jnp.zeros((blk, ATTN_GROUP_WIDTH), F32)
    lse_acc = jnp.zeros((blk, LANES), F32)
    for h in range(ATTN_HEADS_PER_GROUP):
        qh = jnp.where(lane_head == h, q, jnp.zeros_like(q))
        s = lax.dot_general(qh, k, (((1,), (1,)), ((), ())), preferred_element_type=F32)
        s = jnp.where(valid, s * (ATTN_HEAD_DIM ** -0.5), -jnp.inf)
        m = jnp.max(s, axis=-1, keepdims=True)
        p = jnp.exp(s - m)
        den = jnp.sum(p, axis=-1, keepdims=True)
        pv = jnp.dot(p.astype(BF16), v, preferred_element_type=F32)
        o_acc = jnp.where(lane_head == h, pv / den, o_acc)
        lse_acc = jnp.where(lse_lane == h, m + jnp.log(den), lse_acc)
    o_ref[...] = o_acc.astype(o_ref.dtype)
    lse_ref[...] = lse_acc


def _dilated_attention(q, k, v, batch, seq, window, dilation):
    T = q.shape[0]
    d = dilation
    L = seq // d
    nb = L // ATTN_BLOCK
    W = ATTN_GROUP_WIDTH
    view = lambda t, w: t.reshape(batch, L, d * w)
    cur = lambda b, r, i: (b, i, r)
    prev = lambda b, r, i: (b, jnp.maximum(i - 1, 0), r)
    qkv_spec = lambda imap: pl.BlockSpec((None, ATTN_BLOCK, W), imap)
    o, lse = pl.pallas_call(
        functools.partial(_attn_body, span=window // d),
        grid=(batch, d, nb),
        in_specs=[qkv_spec(cur), qkv_spec(prev), qkv_spec(cur), qkv_spec(prev), qkv_spec(cur)],
        out_specs=[qkv_spec(cur), pl.BlockSpec((None, ATTN_BLOCK, LANES), cur)],
        out_shape=[jax.ShapeDtypeStruct((batch, L, d * W), BF16),
                   jax.ShapeDtypeStruct((batch, L, d * LANES), F32)],
        compiler_params=_params("parallel", "parallel", "parallel"),
        name=f"dilated_attention_d{d}",
    )(view(q, W), view(k, W), view(k, W), view(v, W), view(v, W))
    return o.reshape(T, W), lse.reshape(T, LANES)


def _dot_f32(a, b):
    return jnp.dot(a, b, preferred_element_type=F32, precision=HIGHEST)


def _dot_nt_f32(a, b):
    return lax.dot_general(a, b, (((1,), (1,)), ((), ())), preferred_element_type=F32, precision=HIGHEST)


def _dot_tn_f32(a, b):
    return lax.dot_general(a, b, (((0,), (0,)), ((), ())), preferred_element_type=F32, precision=HIGHEST)


def _unit_lower_inverse(m):
    C = m.shape[0]
    bs = C // 4
    ri = lax.broadcasted_iota(jnp.int32, (C, C), 0)
    ci = lax.broadcasted_iota(jnp.int32, (C, C), 1)
    eye = jnp.where(ri == ci, 1.0, 0.0).astype(F32)
    same_block = (ri // bs) == (ci // bs)
    md = jnp.where(same_block, m, 0.0)
    mo = m - md
    p = eye - md
    mk = md
    for _ in range(3):
        mk = _dot_f32(mk, mk)
        p = p + _dot_f32(p, mk)
    n = _dot_f32(p, mo)
    n2 = _dot_f32(n, n)
    imn = eye - n
    q = imn + _dot_f32(imn, n2)
    return _dot_f32(q, p)


def _deltanet_body(halo_ref, x_ref, sm_ref, z_ref, cw_ref, avec_ref, dtb_ref, onw_ref, o_ref, state_ref):
    C = DN_CHUNK
    Dh = DN_HEAD_DIM
    halo_rows = halo_ref.shape[0]
    c_idx = pl.program_id(1)

    @pl.when(c_idx == 0)
    def _():
        state_ref[...] = jnp.zeros_like(state_ref)

    keep_halo = jnp.where(c_idx > 0, 1.0, 0.0).astype(F32)
    xe = jnp.concatenate([halo_ref[...].astype(F32) * keep_halo, x_ref[...].astype(F32)], axis=0)
    cw = cw_ref[...]

    def conv_silu(col):
        xs = xe[:, col * Dh:(col + 1) * Dh]
        acc = xs * cw[DN_CONV - 1:DN_CONV, col * Dh:(col + 1) * Dh]
        for j in range(1, DN_CONV):
            acc = acc + pltpu.roll(xs, j, axis=0) * cw[DN_CONV - 1 - j:DN_CONV - j, col * Dh:(col + 1) * Dh]
        return _silu(acc[halo_rows:])

    def l2n(t):
        return t * lax.rsqrt(jnp.sum(t * t, axis=-1, keepdims=True) + NORM_EPS)

    sm = sm_ref[...]
    beta_all = _sigmoid(sm)
    sp_in = sm + dtb_ref[...]
    softplus = jnp.maximum(sp_in, 0.0) + jnp.log(1.0 + jnp.exp(-jnp.abs(sp_in)))
    g_all = avec_ref[...] * softplus
    rowi = lax.broadcasted_iota(jnp.int32, (C, LANES), 0)
    gc_all = g_all
    s = 1
    while s < C:
        gc_all = gc_all + jnp.where(rowi >= s, pltpu.roll(gc_all, s, axis=0), 0.0)
        s *= 2
    sel = jnp.where(lax.broadcasted_iota(jnp.int32, (SUBLANES_F32, LANES), 1)
                    == lax.broadcasted_iota(jnp.int32, (SUBLANES_F32, LANES), 0) + DN_HEADS, 1.0, 0.0).astype(F32)
    gc_rows = _dot_nt_f32(sel, gc_all)

    ri = lax.broadcasted_iota(jnp.int32, (C, C), 0)
    ci = lax.broadcasted_iota(jnp.int32, (C, C), 1)
    incl = ri >= ci
    strict = ri > ci

    for h in range(DN_HEADS):
        q = l2n(conv_silu(h)) * (Dh ** -0.5)
        k = l2n(conv_silu(DN_HEADS + h))
        v = conv_silu(2 * DN_HEADS + h)
        beta = beta_all[:, h:h + 1]
        gcol = gc_all[:, DN_HEADS + h:DN_HEADS + h + 1]
        grow = gc_rows[h:h + 1, :]
        decay = jnp.exp(jnp.where(incl, gcol - grow, -jnp.inf))
        kb = k * beta
        vb = v * beta
        eg = jnp.exp(gcol)
        m = jnp.where(strict, _dot_nt_f32(kb, k) * decay, 0.0)
        tinv = _unit_lower_inverse(m)
        u = _dot_f32(tinv, vb)
        w = _dot_f32(tinv, kb * eg)
        qk = _dot_nt_f32(q, k) * decay
        g_last = gcol[C - 1:C, :]
        k_dec = k * jnp.exp(g_last - gcol)
        state = state_ref[h]
        v_new = u - _dot_f32(w, state)
        o = _dot_f32(q * eg, state) + _dot_f32(qk, v_new)
        state_ref[h] = state * jnp.exp(g_last) + _dot_tn_f32(k_dec, v_new)
        zg = z_ref[:, h * Dh:(h + 1) * Dh].astype(F32)
        o_ref[:, h * Dh:(h + 1) * Dh] = (_rms_norm(o, onw_ref[...]) * _silu(zg)).astype(o_ref.dtype)


def _deltanet(dn_qkv, small, z, conv_w, a_log, dt_bias, onorm_w, batch, seq):
    T = dn_qkv.shape[0]
    C = DN_CHUNK
    nchunks = seq // C
    halo = SUBLANES_BF16
    lane = jnp.arange(LANES)
    in_decay_lanes = (lane >= DN_HEADS) & (lane < 2 * DN_HEADS)
    idx = jnp.clip(lane - DN_HEADS, 0, DN_HEADS - 1)
    avec = jnp.where(in_decay_lanes, -jnp.exp(a_log.astype(F32))[idx], 0.0).reshape(1, LANES)
    dtb = jnp.where(in_decay_lanes, dt_bias.astype(F32)[idx], 0.0).reshape(1, LANES)
    row = lambda b, c: (b * nchunks + c, 0)
    halo_map = lambda b, c: (jnp.maximum((b * nchunks + c) * (C // halo) - 1, 0), 0)
    return pl.pallas_call(
        _deltanet_body,
        grid=(batch, nchunks),
        in_specs=[pl.BlockSpec((halo, 3 * DN_WIDTH), halo_map),
                  pl.BlockSpec((C, 3 * DN_WIDTH), row),
                  pl.BlockSpec((C, LANES), row),
                  pl.BlockSpec((C, DN_WIDTH), row),
                  _resident((DN_CONV, 3 * DN_WIDTH)), _resident((1, LANES)), _resident((1, LANES)),
                  _resident((1, DN_HEAD_DIM))],
        out_specs=pl.BlockSpec((C, DN_WIDTH), row),
        out_shape=jax.ShapeDtypeStruct((T, DN_WIDTH), BF16),
        scratch_shapes=[pltpu.VMEM((DN_HEADS, DN_HEAD_DIM, DN_HEAD_DIM), F32)],
        compiler_params=_params("parallel", "arbitrary"),
        name="gated_deltanet",
    )(dn_qkv, dn_qkv, small, z, conv_w.astype(F32), avec, dtb, onorm_w.astype(F32).reshape(1, DN_HEAD_DIM))


def _merge_body(o1_ref, o2_ref, o3_ref, l1_ref, l2_ref, l3_ref, ob_ref, gates_ref, x_ref,
                expand_ref, wpa_ref, wpb_ref, wo_ref, h_ref):
    D = x_ref.shape[-1]
    lses = [l1_ref[...], l2_ref[...], l3_ref[...]]
    mx = jnp.maximum(jnp.maximum(lses[0], lses[1]), lses[2])
    es = [jnp.exp(l - mx) for l in lses]
    inv = 1.0 / (es[0] + es[1] + es[2])
    o_a = None
    for e, o_ref in zip(es, (o1_ref, o2_ref, o3_ref)):
        alpha = e * inv
        hi = alpha.astype(BF16)
        lo = (alpha - hi.astype(F32)).astype(BF16)
        alpha_wide = jnp.dot(jnp.concatenate([hi, lo], axis=1), expand_ref[...], preferred_element_type=F32)
        term = alpha_wide * o_ref[...].astype(F32)
        o_a = term if o_a is None else o_a + term
    pa = jnp.dot(o_a.astype(BF16), wpa_ref[...], preferred_element_type=F32)
    pb = jnp.dot(ob_ref[...], wpb_ref[...], preferred_element_type=F32)
    ga = gates_ref[:, :D].astype(F32)
    gb = gates_ref[:, D:].astype(F32)
    y = _sigmoid(ga) * pa + _sigmoid(gb) * pb
    h_ref[...] = x_ref[...] + jnp.dot(y.astype(BF16), wo_ref[...], preferred_element_type=F32)


def _merge(o_groups, lse_groups, o_b, gates, x2d, w_pa, w_pb, w_o):
    T, D = x2d.shape
    W = ATTN_GROUP_WIDTH
    lane = jnp.arange(2 * LANES)[:, None] % LANES
    expand = (lane == (jnp.arange(W)[None, :] // ATTN_HEAD_DIM)).astype(BF16)
    row = lambda i: (i, 0)
    tile = lambda n: pl.BlockSpec((ROW_TILE, n), row)
    return pl.pallas_call(
        _merge_body,
        grid=(T // ROW_TILE,),
        in_specs=[tile(W)] * 3 + [tile(LANES)] * 3 + [tile(DN_WIDTH), tile(2 * D), tile(D),
                  _resident(expand.shape), _resident(w_pa.shape), _resident(w_pb.shape), _resident(w_o.shape)],
        out_specs=tile(D),
        out_shape=jax.ShapeDtypeStruct((T, D), F32),
        compiler_params=_params("parallel"),
        name="merge_out_projection",
    )(*o_groups, *lse_groups, o_b, gates, x2d, expand, w_pa, w_pb, w_o)


def _ffn_body(halo_ref, h_ref, nw_ref, wup_ref, cw_ref, cb_ref, wdown_ref, fw_ref, out_ref, act_ref,
              *, tiles_per_seq, col_tile, final_norm):
    dff = wdown_ref.shape[0]
    halo_rows = halo_ref.shape[0]
    i = pl.program_id(0)
    keep_halo = jnp.where(i % tiles_per_seq > 0, 1.0, 0.0).astype(F32)
    h = h_ref[...]
    hx = jnp.concatenate([halo_ref[...] * keep_halo, h], axis=0)
    hn = _rms_norm(hx, nw_ref[...]).astype(BF16)

    def conv(c0):
        u = jnp.dot(hn, wup_ref[:, c0:c0 + col_tile], preferred_element_type=F32)
        acc = u * cw_ref[FFN_CONV - 1:FFN_CONV, c0:c0 + col_tile]
        for j in range(1, FFN_CONV):
            acc = acc + pltpu.roll(u, j, axis=0) * cw_ref[FFN_CONV - 1 - j:FFN_CONV - j, c0:c0 + col_tile]
        return acc[halo_rows:] + cb_ref[:, c0:c0 + col_tile]

    for c0 in range(0, dff, col_tile):
        act_ref[:, c0:c0 + col_tile] = (_silu(conv(c0)) * conv(dff + c0)).astype(BF16)
    y = h + jnp.dot(act_ref[...], wdown_ref[...], preferred_element_type=F32)
    if final_norm:
        y = _rms_norm(y, fw_ref[...])
    out_ref[...] = y


def _ffn(h2d, seq, norm_w, w_up, conv_w, conv_b, w_down, final_w, final_norm):
    T, D = h2d.shape
    dff = w_down.shape[0]
    halo = SUBLANES_F32
    col_tile = 2 * LANES
    row = lambda i: (i, 0)
    halo_map = lambda i: (jnp.maximum(i * (ROW_TILE // halo) - 1, 0), 0)
    body = functools.partial(_ffn_body, tiles_per_seq=seq // ROW_TILE, col_tile=col_tile, final_norm=final_norm)
    return pl.pallas_call(
        body,
        grid=(T // ROW_TILE,),
        in_specs=[pl.BlockSpec((halo, D), halo_map), pl.BlockSpec((ROW_TILE, D), row), _resident((1, D)),
                  _resident(w_up.shape), _resident((FFN_CONV, 2 * dff)), _resident((1, 2 * dff)),
                  _resident(w_down.shape), _resident((1, D))],
        out_specs=pl.BlockSpec((ROW_TILE, D), row),
        out_shape=jax.ShapeDtypeStruct((T, D), F32),
        scratch_shapes=[pltpu.VMEM((ROW_TILE, dff), BF16)],
        compiler_params=_params("parallel"),
        name="conv_glu_ffn",
    )(h2d, h2d, norm_w.reshape(1, D), w_up, conv_w.astype(F32), conv_b.astype(F32).reshape(1, 2 * dff),
      w_down, final_w.reshape(1, D))


def _pack_in_weights(w_in):
    D = w_in.shape[0]
    ng = len(ATTN_GROUPS)
    aw = ng * ATTN_GROUP_WIDTH
    W = ATTN_GROUP_WIDTH
    cols = []
    for g in range(ng):
        for part in range(3):
            cols.append(w_in[:, part * aw + g * W: part * aw + (g + 1) * W])
    c = 3 * aw
    dn = w_in[:, c:c + 3 * DN_WIDTH]
    c += 3 * DN_WIDTH
    small = w_in[:, c:c + 2 * DN_HEADS]
    c += 2 * DN_HEADS
    z = w_in[:, c:c + DN_WIDTH]
    c += DN_WIDTH
    gates = w_in[:, c:]
    small = jnp.pad(small, ((0, 0), (0, LANES - 2 * DN_HEADS)))
    packed = jnp.concatenate(cols + [dn, z, gates, small], axis=1).astype(BF16)
    widths = [W] * (3 * ng) + [3 * DN_WIDTH, DN_WIDTH, 2 * D, LANES]
    dtypes = [BF16] * (3 * ng + 3) + [F32]
    return packed, widths, dtypes


def kernel(x, norm1_w, w_in, dn_conv_w, dn_a_log, dn_dt_bias, dn_onorm_w, w_pa, w_pb, w_o, norm2_w, w_up,
           ffn_conv_w, ffn_conv_b, w_down, final_norm_w):
    B, S, D = x.shape
    depth = w_in.shape[0]
    assert S % ROW_TILE == 0 and S % (ATTN_BLOCK * ATTN_GROUPS[-1][1]) == 0
    xf = x.astype(F32).reshape(B * S, D)
    for l in range(depth):
        packed, widths, dtypes = _pack_in_weights(w_in[l])
        outs = _in_projection(xf, norm1_w[l].astype(F32), packed, widths, dtypes)
        ng = len(ATTN_GROUPS)
        o_groups, lse_groups = [], []
        for g, (window, dilation) in enumerate(ATTN_GROUPS):
            q, k, v = outs[3 * g:3 * g + 3]
            o_g, lse_g = _dilated_attention(q, k, v, B, S, window, dilation)
            o_groups.append(o_g)
            lse_groups.append(lse_g)
        dn_qkv, z, gates, small = outs[3 * ng:]
        o_b = _deltanet(dn_qkv, small, z, dn_conv_w[l], dn_a_log[l], dn_dt_bias[l], dn_onorm_w[l], B, S)
        h = _merge(o_groups, lse_groups, o_b, gates, xf, w_pa[l].astype(BF16), w_pb[l].astype(BF16),
                   w_o[l].astype(BF16))
        xf = _ffn(h, S, norm2_w[l].astype(F32), w_up[l].astype(BF16), ffn_conv_w[l], ffn_conv_b[l],
                  w_down[l].astype(BF16), final_norm_w.astype(F32), final_norm=(l == depth - 1))
    return xf.reshape(B, S, D).astype(x.dtype)
```

```python
import functools

import jax
import jax.numpy as jnp
from jax import lax
from jax.experimental import pallas as pl
from jax.experimental.pallas import tpu as pltpu

F32 = jnp.float32
BF16 = jnp.bfloat16
HIGHEST = lax.Precision.HIGHEST

NORM_EPS = 1e-6

ATTN_HEAD_DIM = 64
ATTN_HEADS_PER_GROUP = 4
ATTN_GROUP_WIDTH = ATTN_HEADS_PER_GROUP * ATTN_HEAD_DIM
ATTN_GROUPS = ((128, 1), (512, 4), (2048, 16))
ATTN_BLOCK = 128
DN_HEADS = 4
DN_HEAD_DIM = 128
DN_WIDTH = DN_HEADS * DN_HEAD_DIM
DN_CONV = 4
DN_CHUNK = 64
FFN_CONV = 3

LANES = 128
SUBLANES_F32 = 8
SUBLANES_BF16 = 16
VMEM_LIMIT_BYTES = 56 * 1024 * 1024

ROW_TILE = 512


def _resident(shape):
    nd = len(shape)
    return pl.BlockSpec(shape, lambda *_: (0,) * nd, pipeline_mode=pl.Buffered(1))


def _params(*semantics):
    return pltpu.CompilerParams(dimension_semantics=semantics, vmem_limit_bytes=VMEM_LIMIT_BYTES)


def _rms_norm(x, w):
    return x * lax.rsqrt(jnp.mean(x * x, axis=-1, keepdims=True) + NORM_EPS) * w


def _silu(x):
    return x * (1.0 / (1.0 + jnp.exp(-x)))


def _sigmoid(x):
    return 1.0 / (1.0 + jnp.exp(-x))


def _inproj_body(x_ref, nw_ref, w_ref, *out_refs):
    xn = _rms_norm(x_ref[...], nw_ref[...]).astype(BF16)
    c0 = 0
    for o_ref in out_refs:
        n = o_ref.shape[-1]
        o_ref[...] = jnp.dot(xn, w_ref[:, c0:c0 + n], preferred_element_type=F32).astype(o_ref.dtype)
        c0 += n


def _in_projection(x2d, norm_w, w_packed, out_widths, out_dtypes):
    T, D = x2d.shape
    grid = (T // ROW_TILE,)
    row = lambda i: (i, 0)
    return pl.pallas_call(
        _inproj_body,
        grid=grid,
        in_specs=[pl.BlockSpec((ROW_TILE, D), row), _resident((1, D)), _resident(w_packed.shape)],
        out_specs=[pl.BlockSpec((ROW_TILE, n), row) for n in out_widths],
        out_shape=[jax.ShapeDtypeStruct((T, n), dt) for n, dt in zip(out_widths, out_dtypes)],
        compiler_params=_params("parallel"),
        name="in_projection",
    )(x2d, norm_w.reshape(1, D), w_packed)


def _attn_body(q_ref, kp_ref, kc_ref, vp_ref, vc_ref, o_ref, lse_ref, *, span):
    blk = ATTN_BLOCK
    i = pl.program_id(2)
    q = q_ref[...]
    k = jnp.concatenate([kp_ref[...], kc_ref[...]], axis=0)
    v = jnp.concatenate([vp_ref[...], vc_ref[...]], axis=0)
    row = lax.broadcasted_iota(jnp.int32, (blk, 2 * blk), 0)
    col = lax.broadcasted_iota(jnp.int32, (blk, 2 * blk), 1)
    rel = row + blk - col
    first_key = jnp.where(i > 0, 0, blk)
    valid = (rel >= 0) & (rel <= span) & (col >= first_key)
    lane_head = lax.broadcasted_iota(jnp.int32, (blk, ATTN_GROUP_WIDTH), 1) // ATTN_HEAD_DIM
    lse_lane = lax.broadcasted_iota(jnp.int32, (blk, LANES), 1)
    o_acc = jnp.zeros((blk, ATTN_GROUP_WIDTH), F32)
    lse_acc = jnp.zeros((blk, LANES), F32)
    for h in range(ATTN_HEADS_PER_GROUP):
        qh = jnp.where(lane_head == h, q, jnp.zeros_like(q))
        s = lax.dot_general(qh, k, (((1,), (1,)), ((), ())), preferred_element_type=F32)
        s = jnp.where(valid, s * (ATTN_HEAD_DIM ** -0.5), -jnp.inf)
        m = jnp.max(s, axis=-1, keepdims=True)
        p = jnp.exp(s - m)
        den = jnp.sum(p, axis=-1, keepdims=True)
        pv = jnp.dot(p.astype(BF16), v, preferred_element_type=F32)
        o_acc = jnp.where(lane_head == h, pv / den, o_acc)
        lse_acc = jnp.where(lse_lane == h, m + jnp.log(den), lse_acc)
    o_ref[...] = o_acc.astype(o_ref.dtype)
    lse_ref[...] = lse_acc


def _dilated_attention(q, k, v, batch, seq, window, dilation):
    T = q.shape[0]
    d = dilation
    L = seq // d
    nb = L // ATTN_BLOCK
    W = ATTN_GROUP_WIDTH
    view = lambda t, w: t.reshape(batch, L, d * w)
    cur = lambda b, r, i: (b, i, r)
    prev = lambda b, r, i: (b, jnp.maximum(i - 1, 0), r)
    qkv_spec = lambda imap: pl.BlockSpec((None, ATTN_BLOCK, W), imap)
    o, lse = pl.pallas_call(
        functools.partial(_attn_body, span=window // d),
        grid=(batch, d, nb),
        in_specs=[qkv_spec(cur), qkv_spec(prev), qkv_spec(cur), qkv_spec(prev), qkv_spec(cur)],
        out_specs=[qkv_spec(cur), pl.BlockSpec((None, ATTN_BLOCK, LANES), cur)],
        out_shape=[jax.ShapeDtypeStruct((batch, L, d * W), BF16),
                   jax.ShapeDtypeStruct((batch, L, d * LANES), F32)],
        compiler_params=_params("parallel", "parallel", "parallel"),
        name=f"dilated_attention_d{d}",
    )(view(q, W), view(k, W), view(k, W), view(v, W), view(v, W))
    return o.reshape(T, W), lse.reshape(T, LANES)


def _dot_f32(a, b):
    return jnp.dot(a.astype(BF16), b.astype(BF16), preferred_element_type=F32)


def _dot_nt_f32(a, b):
    return lax.dot_general(a.astype(BF16), b.astype(BF16), (((1,), (1,)), ((), ())), preferred_element_type=F32)


def _dot_tn_f32(a, b):
    return lax.dot_general(a.astype(BF16), b.astype(BF16), (((0,), (0,)), ((), ())), preferred_element_type=F32)


def _unit_lower_inverse_minus_eye(m):
    C = m.shape[0]
    bs = C // 4
    ri = lax.broadcasted_iota(jnp.int32, (C, C), 0)
    ci = lax.broadcasted_iota(jnp.int32, (C, C), 1)
    same_block = (ri // bs) == (ci // bs)
    md = jnp.where(same_block, m, 0.0)
    mo = m - md
    p = -md
    mk = md
    for _ in range(3):
        mk = _dot_f32(mk, mk)
        p = p + mk + _dot_f32(p, mk)
    n = mo + _dot_f32(p, mo)
    n2 = _dot_f32(n, n)
    q = n2 - n - _dot_f32(n, n2)
    return q + p + _dot_f32(q, p)


def _deltanet_body(halo_ref, x_ref, sm_ref, z_ref, cw_ref, avec_ref, dtb_ref, onw_ref, o_ref, state_ref):
    C = DN_CHUNK
    Dh = DN_HEAD_DIM
    halo_rows = halo_ref.shape[0]
    c_idx = pl.program_id(1)

    @pl.when(c_idx == 0)
    def _():
        state_ref[...] = jnp.zeros_like(state_ref)

    keep_halo = jnp.where(c_idx > 0, 1.0, 0.0).astype(F32)
    xe = jnp.concatenate([halo_ref[...].astype(F32) * keep_halo, x_ref[...].astype(F32)], axis=0)
    cw = cw_ref[...]

    def conv_silu(col):
        xs = xe[:, col * Dh:(col + 1) * Dh]
        acc = xs * cw[DN_CONV - 1:DN_CONV, col * Dh:(col + 1) * Dh]
        for j in range(1, DN_CONV):
            acc = acc + pltpu.roll(xs, j, axis=0) * cw[DN_CONV - 1 - j:DN_CONV - j, col * Dh:(col + 1) * Dh]
        return _silu(acc[halo_rows:])

    def l2n(t):
        return t * lax.rsqrt(jnp.sum(t * t, axis=-1, keepdims=True) + NORM_EPS)

    sm = sm_ref[...]
    beta_all = _sigmoid(sm)
    sp_in = sm + dtb_ref[...]
    softplus = jnp.maximum(sp_in, 0.0) + jnp.log(1.0 + jnp.exp(-jnp.abs(sp_in)))
    g_all = avec_ref[...] * softplus
    rowi = lax.broadcasted_iota(jnp.int32, (C, LANES), 0)
    gc_all = g_all
    s = 1
    while s < C:
        gc_all = gc_all + jnp.where(rowi >= s, pltpu.roll(gc_all, s, axis=0), 0.0)
        s *= 2
    sel = jnp.where(lax.broadcasted_iota(jnp.int32, (SUBLANES_F32, LANES), 1)
                    == lax.broadcasted_iota(jnp.int32, (SUBLANES_F32, LANES), 0) + DN_HEADS, 1.0, 0.0).astype(F32)
    gc_rows = lax.dot_general(sel, gc_all, (((1,), (1,)), ((), ())), preferred_element_type=F32,
                              precision=HIGHEST)

    ri = lax.broadcasted_iota(jnp.int32, (C, C), 0)
    ci = lax.broadcasted_iota(jnp.int32, (C, C), 1)
    incl = ri >= ci
    strict = ri > ci

    for h in range(DN_HEADS):
        q = l2n(conv_silu(h)) * (Dh ** -0.5)
        k = l2n(conv_silu(DN_HEADS + h))
        v = conv_silu(2 * DN_HEADS + h)
        beta = beta_all[:, h:h + 1]
        gcol = gc_all[:, DN_HEADS + h:DN_HEADS + h + 1]
        grow = gc_rows[h:h + 1, :]
        decay = jnp.exp(jnp.where(incl, gcol - grow, -jnp.inf))
        kb = k * beta
        vb = v * beta
        eg = jnp.exp(gcol)
        m = jnp.where(strict, _dot_nt_f32(kb, k) * decay, 0.0)
        tinv = _unit_lower_inverse_minus_eye(m)
        kbe = kb * eg
        u = vb + _dot_f32(tinv, vb)
        w = kbe + _dot_f32(tinv, kbe)
        qk = _dot_nt_f32(q, k) * decay
        g_last = gcol[C - 1:C, :]
        k_dec = k * jnp.exp(g_last - gcol)
        state = state_ref[h]
        v_new = u - _dot_f32(w, state)
        o = _dot_f32(q * eg, state) + _dot_f32(qk, v_new)
        state_ref[h] = state * jnp.exp(g_last) + _dot_tn_f32(k_dec, v_new)
        zg = z_ref[:, h * Dh:(h + 1) * Dh].astype(F32)
        o_ref[:, h * Dh:(h + 1) * Dh] = (_rms_norm(o, onw_ref[...]) * _silu(zg)).astype(o_ref.dtype)


def _deltanet(dn_qkv, small, z, conv_w, a_log, dt_bias, onorm_w, batch, seq):
    T = dn_qkv.shape[0]
    C = DN_CHUNK
    nchunks = seq // C
    halo = SUBLANES_BF16
    lane = jnp.arange(LANES)
    in_decay_lanes = (lane >= DN_HEADS) & (lane < 2 * DN_HEADS)
    idx = jnp.clip(lane - DN_HEADS, 0, DN_HEADS - 1)
    avec = jnp.where(in_decay_lanes, -jnp.exp(a_log.astype(F32))[idx], 0.0).reshape(1, LANES)
    dtb = jnp.where(in_decay_lanes, dt_bias.astype(F32)[idx], 0.0).reshape(1, LANES)
    row = lambda b, c: (b * nchunks + c, 0)
    halo_map = lambda b, c: (jnp.maximum((b * nchunks + c) * (C // halo) - 1, 0), 0)
    return pl.pallas_call(
        _deltanet_body,
        grid=(batch, nchunks),
        in_specs=[pl.BlockSpec((halo, 3 * DN_WIDTH), halo_map),
                  pl.BlockSpec((C, 3 * DN_WIDTH), row),
                  pl.BlockSpec((C, LANES), row),
                  pl.BlockSpec((C, DN_WIDTH), row),
                  _resident((DN_CONV, 3 * DN_WIDTH)), _resident((1, LANES)), _resident((1, LANES)),
                  _resident((1, DN_HEAD_DIM))],
        out_specs=pl.BlockSpec((C, DN_WIDTH), row),
        out_shape=jax.ShapeDtypeStruct((T, DN_WIDTH), BF16),
        scratch_shapes=[pltpu.VMEM((DN_HEADS, DN_HEAD_DIM, DN_HEAD_DIM), F32)],
        compiler_params=_params("parallel", "arbitrary"),
        name="gated_deltanet",
    )(dn_qkv, dn_qkv, small, z, conv_w.astype(F32), avec, dtb, onorm_w.astype(F32).reshape(1, DN_HEAD_DIM))


def _merge_body(o1_ref, o2_ref, o3_ref, l1_ref, l2_ref, l3_ref, ob_ref, gates_ref, x_ref,
                expand_ref, wpa_ref, wpb_ref, wo_ref, h_ref):
    D = x_ref.shape[-1]
    lses = [l1_ref[...], l2_ref[...], l3_ref[...]]
    mx = jnp.maximum(jnp.maximum(lses[0], lses[1]), lses[2])
    es = [jnp.exp(l - mx) for l in lses]
    inv = 1.0 / (es[0] + es[1] + es[2])
    o_a = None
    for e, o_ref in zip(es, (o1_ref, o2_ref, o3_ref)):
        alpha = e * inv
        hi = alpha.astype(BF16)
        lo = (alpha - hi.astype(F32)).astype(BF16)
        alpha_wide = jnp.dot(jnp.concatenate([hi, lo], axis=1), expand_ref[...], preferred_element_type=F32)
        term = alpha_wide * o_ref[...].astype(F32)
        o_a = term if o_a is None else o_a + term
    pa = jnp.dot(o_a.astype(BF16), wpa_ref[...], preferred_element_type=F32)
    pb = jnp.dot(ob_ref[...], wpb_ref[...], preferred_element_type=F32)
    ga = gates_ref[:, :D].astype(F32)
    gb = gates_ref[:, D:].astype(F32)
    y = _sigmoid(ga) * pa + _sigmoid(gb) * pb
    h_ref[...] = x_ref[...] + jnp.dot(y.astype(BF16), wo_ref[...], preferred_element_type=F32)


def _merge(o_groups, lse_groups, o_b, gates, x2d, w_pa, w_pb, w_o):
    T, D = x2d.shape
    W = ATTN_GROUP_WIDTH
    lane = jnp.arange(2 * LANES)[:, None] % LANES
    expand = (lane == (jnp.arange(W)[None, :] // ATTN_HEAD_DIM)).astype(BF16)
    row = lambda i: (i, 0)
    tile = lambda n: pl.BlockSpec((ROW_TILE, n), row)
    return pl.pallas_call(
        _merge_body,
        grid=(T // ROW_TILE,),
        in_specs=[tile(W)] * 3 + [tile(LANES)] * 3 + [tile(DN_WIDTH), tile(2 * D), tile(D),
                  _resident(expand.shape), _resident(w_pa.shape), _resident(w_pb.shape), _resident(w_o.shape)],
        out_specs=tile(D),
        out_shape=jax.ShapeDtypeStruct((T, D), F32),
        compiler_params=_params("parallel"),
        name="merge_out_projection",
    )(*o_groups, *lse_groups, o_b, gates, x2d, expand, w_pa, w_pb, w_o)


def _ffn_body(halo_ref, h_ref, nw_ref, wup_ref, cw_ref, cb_ref, wdown_ref, fw_ref, out_ref, act_ref,
              *, tiles_per_seq, col_tile, final_norm):
    dff = wdown_ref.shape[0]
    halo_rows = halo_ref.shape[0]
    i = pl.program_id(0)
    keep_halo = jnp.where(i % tiles_per_seq > 0, 1.0, 0.0).astype(F32)
    h = h_ref[...]
    hx = jnp.concatenate([halo_ref[...] * keep_halo, h], axis=0)
    hn = _rms_norm(hx, nw_ref[...]).astype(BF16)

    def conv(c0):
        u = jnp.dot(hn, wup_ref[:, c0:c0 + col_tile], preferred_element_type=F32)
        acc = u * cw_ref[FFN_CONV - 1:FFN_CONV, c0:c0 + col_tile]
        for j in range(1, FFN_CONV):
            acc = acc + pltpu.roll(u, j, axis=0) * cw_ref[FFN_CONV - 1 - j:FFN_CONV - j, c0:c0 + col_tile]
        return acc[halo_rows:] + cb_ref[:, c0:c0 + col_tile]

    for c0 in range(0, dff, col_tile):
        act_ref[:, c0:c0 + col_tile] = (_silu(conv(c0)) * conv(dff + c0)).astype(BF16)
    y = h + jnp.dot(act_ref[...], wdown_ref[...], preferred_element_type=F32)
    if final_norm:
        y = _rms_norm(y, fw_ref[...])
    out_ref[...] = y


def _ffn(h2d, seq, norm_w, w_up, conv_w, conv_b, w_down, final_w, final_norm):
    T, D = h2d.shape
    dff = w_down.shape[0]
    halo = SUBLANES_F32
    col_tile = 2 * LANES
    row = lambda i: (i, 0)
    halo_map = lambda i: (jnp.maximum(i * (ROW_TILE // halo) - 1, 0), 0)
    body = functools.partial(_ffn_body, tiles_per_seq=seq // ROW_TILE, col_tile=col_tile, final_norm=final_norm)
    return pl.pallas_call(
        body,
        grid=(T // ROW_TILE,),
        in_specs=[pl.BlockSpec((halo, D), halo_map), pl.BlockSpec((ROW_TILE, D), row), _resident((1, D)),
                  _resident(w_up.shape), _resident((FFN_CONV, 2 * dff)), _resident((1, 2 * dff)),
                  _resident(w_down.shape), _resident((1, D))],
        out_specs=pl.BlockSpec((ROW_TILE, D), row),
        out_shape=jax.ShapeDtypeStruct((T, D), F32),
        scratch_shapes=[pltpu.VMEM((ROW_TILE, dff), BF16)],
        compiler_params=_params("parallel"),
        name="conv_glu_ffn",
    )(h2d, h2d, norm_w.reshape(1, D), w_up, conv_w.astype(F32), conv_b.astype(F32).reshape(1, 2 * dff),
      w_down, final_w.reshape(1, D))


def _pack_in_weights(w_in):
    D = w_in.shape[0]
    ng = len(ATTN_GROUPS)
    aw = ng * ATTN_GROUP_WIDTH
    W = ATTN_GROUP_WIDTH
    cols = []
    for g in range(ng):
        for part in range(3):
            cols.append(w_in[:, part * aw + g * W: part * aw + (g + 1) * W])
    c = 3 * aw
    dn = w_in[:, c:c + 3 * DN_WIDTH]
    c += 3 * DN_WIDTH
    small = w_in[:, c:c + 2 * DN_HEADS]
    c += 2 * DN_HEADS
    z = w_in[:, c:c + DN_WIDTH]
    c += DN_WIDTH
    gates = w_in[:, c:]
    small = jnp.pad(small, ((0, 0), (0, LANES - 2 * DN_HEADS)))
    packed = jnp.concatenate(cols + [dn, z, gates, small], axis=1).astype(BF16)
    widths = [W] * (3 * ng) + [3 * DN_WIDTH, DN_WIDTH, 2 * D, LANES]
    dtypes = [BF16] * (3 * ng + 3) + [F32]
    return packed, widths, dtypes


def kernel(x, norm1_w, w_in, dn_conv_w, dn_a_log, dn_dt_bias, dn_onorm_w, w_pa, w_pb, w_o, norm2_w, w_up,
           ffn_conv_w, ffn_conv_b, w_down, final_norm_w):
    B, S, D = x.shape
    depth = w_in.shape[0]
    assert S % ROW_TILE == 0 and S % (ATTN_BLOCK * ATTN_GROUPS[-1][1]) == 0
    xf = x.astype(F32).reshape(B * S, D)
    for l in range(depth):
        packed, widths, dtypes = _pack_in_weights(w_in[l])
        outs = _in_projection(xf, norm1_w[l].astype(F32), packed, widths, dtypes)
        ng = len(ATTN_GROUPS)
        o_groups, lse_groups = [], []
        for g, (window, dilation) in enumerate(ATTN_GROUPS):
            q, k, v = outs[3 * g:3 * g + 3]
            o_g, lse_g = _dilated_attention(q, k, v, B, S, window, dilation)
            o_groups.append(o_g)
            lse_groups.append(lse_g)
        dn_qkv, z, gates, small = outs[3 * ng:]
        o_b = _deltanet(dn_qkv, small, z, dn_conv_w[l], dn_a_log[l], dn_dt_bias[l], dn_onorm_w[l], B, S)
        h = _merge(o_groups, lse_groups, o_b, gates, xf, w_pa[l].astype(BF16), w_pb[l].astype(BF16),
                   w_o[l].astype(BF16))
        xf = _ffn(h, S, norm2_w[l].astype(F32), w_up[l].astype(BF16), ffn_conv_w[l], ffn_conv_b[l],
                  w_down[l].astype(BF16), final_norm_w.astype(F32), final_norm=(l == depth - 1))
    return xf.reshape(B, S, D).astype(x.dtype)
```

```python
import functools

import jax
import jax.numpy as jnp
from jax import lax
from jax.experimental import pallas as pl
from jax.experimental.pallas import tpu as pltpu

F32 = jnp.float32
BF16 = jnp.bfloat16
HIGHEST = lax.Precision.HIGHEST

NORM_EPS = 1e-6

ATTN_HEAD_DIM = 64
ATTN_HEADS_PER_GROUP = 4
ATTN_GROUP_WIDTH = ATTN_HEADS_PER_GROUP * ATTN_HEAD_DIM
ATTN_GROUPS = ((128, 1), (512, 4), (2048, 16))
ATTN_BLOCK = 128
ATTN_Q_BLOCKS = 2
DN_HEADS = 4
DN_HEAD_DIM = 128
DN_WIDTH = DN_HEADS * DN_HEAD_DIM
DN_CONV = 4
DN_CHUNK = 64
DN_BLOCK_CHUNKS = 4
DN_SOLVE_BLOCK = 16
FFN_CONV = 3

LANES = 128
SUBLANES_F32 = 8
SUBLANES_BF16 = 16
VMEM_LIMIT_BYTES = 56 * 1024 * 1024

ROW_TILE = 512


def _resident(shape):
    nd = len(shape)
    return pl.BlockSpec(shape, lambda *_: (0,) * nd, pipeline_mode=pl.Buffered(1))


def _params(*semantics):
    return pltpu.CompilerParams(dimension_semantics=semantics, vmem_limit_bytes=VMEM_LIMIT_BYTES)


def _rms_norm(x, w):
    return x * lax.rsqrt(jnp.mean(x * x, axis=-1, keepdims=True) + NORM_EPS) * w


def _silu(x):
    return x * (1.0 / (1.0 + jnp.exp(-x)))


def _sigmoid(x):
    return 1.0 / (1.0 + jnp.exp(-x))


def _to_dilated(y, relay_ref, o_ref, d):
    rows = y.shape[0]
    for half in range(ATTN_GROUP_WIDTH // LANES):
        relay_ref[half] = y[:, half * LANES:(half + 1) * LANES]
    for r in range(d):
        for half in range(ATTN_GROUP_WIDTH // LANES):
            c0 = r * ATTN_GROUP_WIDTH + half * LANES
            o_ref[:, c0:c0 + LANES] = relay_ref[half, pl.ds(r, rows // d, stride=d), :].astype(o_ref.dtype)


def _from_dilated(x_ref, relay_ref, d):
    rows = relay_ref.shape[1]
    nslab = relay_ref.shape[0]
    width = nslab * LANES
    for r in range(d):
        for slab in range(nslab):
            c0 = r * width + slab * LANES
            relay_ref[slab, pl.ds(r, rows // d, stride=d), :] = x_ref[:, c0:c0 + LANES].astype(F32)
    return jnp.concatenate([relay_ref[slab] for slab in range(nslab)], axis=1)


def _inproj_body(x_ref, nw_ref, w_ref, *refs, dilations):
    out_refs = refs[:len(dilations)]
    relay_refs = refs[len(dilations):]
    xn = _rms_norm(x_ref[...], nw_ref[...]).astype(BF16)
    c0 = 0
    n_relay = 0
    for o_ref, d in zip(out_refs, dilations):
        n = o_ref.shape[-1] // d
        y = jnp.dot(xn, w_ref[:, c0:c0 + n], preferred_element_type=F32)
        if d == 1:
            o_ref[...] = y.astype(o_ref.dtype)
        else:
            _to_dilated(y, relay_refs[n_relay], o_ref, d)
            n_relay += 1
        c0 += n


def _in_projection(x2d, norm_w, w_packed, out_widths, out_dtypes, dilations):
    T, D = x2d.shape
    grid = (T // ROW_TILE,)
    row = lambda i: (i, 0)
    n_relay = sum(1 for d in dilations if d > 1)
    return pl.pallas_call(
        functools.partial(_inproj_body, dilations=tuple(dilations)),
        grid=grid,
        in_specs=[pl.BlockSpec((ROW_TILE, D), row), _resident((1, D)), _resident(w_packed.shape)],
        out_specs=[pl.BlockSpec((ROW_TILE // d, d * n), row) for n, d in zip(out_widths, dilations)],
        out_shape=[jax.ShapeDtypeStruct((T // d, d * n), dt)
                   for n, dt, d in zip(out_widths, out_dtypes, dilations)],
        scratch_shapes=[pltpu.VMEM((ATTN_GROUP_WIDTH // LANES, ROW_TILE, LANES), F32)] * n_relay,
        compiler_params=_params("parallel"),
        name="in_projection",
    )(x2d, norm_w.reshape(1, D), w_packed)


def _attn_body(q_ref, kp_ref, kc_ref, vp_ref, vc_ref, o_ref, lse_ref, *, span):
    blk = ATTN_BLOCK
    nq = q_ref.shape[0] // blk
    H = ATTN_HEADS_PER_GROUP
    i = pl.program_id(2)
    kcat = jnp.concatenate([kp_ref[...], kc_ref[...]], axis=0)
    vcat = jnp.concatenate([vp_ref[...], vc_ref[...]], axis=0)
    row = lax.broadcasted_iota(jnp.int32, (blk, 2 * blk), 0)
    col = lax.broadcasted_iota(jnp.int32, (blk, 2 * blk), 1)
    rel = row + blk - col
    band = (rel >= 0) & (rel <= span)
    first_key = jnp.where(i > 0, 0, blk)
    band_first = band & (col >= first_key)
    lane_head = lax.broadcasted_iota(jnp.int32, (blk, ATTN_GROUP_WIDTH), 1) // ATTN_HEAD_DIM
    lse_lane = lax.broadcasted_iota(jnp.int32, (blk, LANES), 1)

    scores = []
    for jb in range(nq):
        q = q_ref[jb * blk:(jb + 1) * blk, :] * (ATTN_HEAD_DIM ** -0.5)
        qs = jnp.concatenate([jnp.where(lane_head == h, q, jnp.zeros_like(q)) for h in range(H)], axis=0)
        scores.append(lax.dot_general(qs, kcat[jb * blk:(jb + 2) * blk], (((1,), (1,)), ((), ())),
                                      preferred_element_type=F32))
    probs, maxes, dens = [], [], []
    for jb in range(nq):
        valid = band_first if jb == 0 else band
        for h in range(H):
            s = jnp.where(valid, scores[jb][h * blk:(h + 1) * blk], -jnp.inf)
            m = jnp.max(s, axis=-1, keepdims=True)
            p = jnp.exp(s - m)
            probs.append(p.astype(BF16))
            maxes.append(m)
            dens.append(jnp.sum(p, axis=-1, keepdims=True))
    for jb in range(nq):
        pv = jnp.dot(jnp.concatenate(probs[jb * H:(jb + 1) * H], axis=0), vcat[jb * blk:(jb + 2) * blk],
                     preferred_element_type=F32)
        o_acc = jnp.zeros((blk, ATTN_GROUP_WIDTH), F32)
        lse_acc = jnp.zeros((blk, LANES), F32)
        for h in range(H):
            den = dens[jb * H + h]
            o_acc = jnp.where(lane_head == h, pv[h * blk:(h + 1) * blk] / den, o_acc)
            lse_acc = jnp.where(lse_lane == h, maxes[jb * H + h] + jnp.log(den), lse_acc)
        o_ref[jb * blk:(jb + 1) * blk, :] = o_acc.astype(o_ref.dtype)
        lse_ref[jb * blk:(jb + 1) * blk, :] = lse_acc


def _dilated_attention(q, k, v, batch, seq, window, dilation):
    d = dilation
    L = seq // d
    W = ATTN_GROUP_WIDTH
    qrows = ATTN_Q_BLOCKS * ATTN_BLOCK
    view = lambda t: t.reshape(batch, L, t.shape[-1])
    cur = lambda b, r, i: (b, i, r)
    prev = lambda b, r, i: (b, jnp.maximum(i * ATTN_Q_BLOCKS - 1, 0), r)
    cur_spec = pl.BlockSpec((None, qrows, W), cur)
    prev_spec = pl.BlockSpec((None, ATTN_BLOCK, W), prev)
    o, lse = pl.pallas_call(
        functools.partial(_attn_body, span=window // d),
        grid=(batch, d, L // qrows),
        in_specs=[cur_spec, prev_spec, cur_spec, prev_spec, cur_spec],
        out_specs=[cur_spec, pl.BlockSpec((None, qrows, LANES), cur)],
        out_shape=[jax.ShapeDtypeStruct((batch, L, d * W), BF16),
                   jax.ShapeDtypeStruct((batch, L, d * LANES), F32)],
        compiler_params=_params("parallel", "parallel", "parallel"),
        name=f"dilated_attention_d{d}",
    )(view(q), view(k), view(k), view(v), view(v))
    return o.reshape(batch * L, d * W), lse.reshape(batch * L, d * LANES)


def _dot_f32(a, b):
    return jnp.dot(a.astype(BF16), b.astype(BF16), preferred_element_type=F32)


def _dot_nt_f32(a, b):
    return lax.dot_general(a.astype(BF16), b.astype(BF16), (((1,), (1,)), ((), ())), preferred_element_type=F32)


def _dot_tn_f32(a, b):
    return lax.dot_general(a.astype(BF16), b.astype(BF16), (((0,), (0,)), ((), ())), preferred_element_type=F32)


def _deltanet_body(halo_ref, x_ref, sm_ref, z_ref, cw_ref, avec_ref, dtb_ref, onw_ref, o_ref,
                   state_ref, qkv_ref):
    C = DN_CHUNK
    Dh = DN_HEAD_DIM
    halo_rows = halo_ref.shape[0]
    rows = x_ref.shape[0]
    blk = pl.program_id(1)

    @pl.when(blk == 0)
    def _():
        state_ref[...] = jnp.zeros_like(state_ref)

    def l2n(t):
        return t * lax.rsqrt(jnp.sum(t * t, axis=-1, keepdims=True) + NORM_EPS)

    keep_halo = jnp.where(blk > 0, 1.0, 0.0).astype(F32)
    for col in range(3 * DN_HEADS):
        lanes = slice(col * Dh, (col + 1) * Dh)
        xs = jnp.concatenate([halo_ref[:, lanes].astype(F32) * keep_halo, x_ref[:, lanes].astype(F32)], axis=0)
        acc = xs * cw_ref[DN_CONV - 1:DN_CONV, lanes]
        for j in range(1, DN_CONV):
            acc = acc + pltpu.roll(xs, j, axis=0) * cw_ref[DN_CONV - 1 - j:DN_CONV - j, lanes]
        y = _silu(acc[halo_rows:])
        if col < DN_HEADS:
            y = l2n(y) * (Dh ** -0.5)
        elif col < 2 * DN_HEADS:
            y = l2n(y)
        qkv_ref[:, lanes] = y

    sm = sm_ref[...]
    beta_all = _sigmoid(sm)
    sp_in = sm + dtb_ref[...]
    softplus = jnp.maximum(sp_in, 0.0) + jnp.log(1.0 + jnp.exp(-jnp.abs(sp_in)))
    g_all = avec_ref[...] * softplus
    row_in_chunk = lax.broadcasted_iota(jnp.int32, (rows, LANES), 0) % C
    gc_all = g_all
    s = 1
    while s < C:
        gc_all = gc_all + jnp.where(row_in_chunk >= s, pltpu.roll(gc_all, s, axis=0), 0.0)
        s *= 2
    sel = jnp.where(lax.broadcasted_iota(jnp.int32, (SUBLANES_F32, LANES), 1)
                    == lax.broadcasted_iota(jnp.int32, (SUBLANES_F32, LANES), 0) + DN_HEADS, 1.0, 0.0).astype(F32)

    ri = lax.broadcasted_iota(jnp.int32, (C, C), 0)
    ci = lax.broadcasted_iota(jnp.int32, (C, C), 1)
    incl = ri >= ci
    strict = ri > ci
    same_block = (ri // DN_SOLVE_BLOCK) == (ci // DN_SOLVE_BLOCK)
    nc = rows // C

    items = []
    for c in range(nc):
        rs = slice(c * C, (c + 1) * C)
        gc_c = gc_all[rs]
        gc_rows = lax.dot_general(sel, gc_c, (((1,), (1,)), ((), ())), preferred_element_type=F32,
                                  precision=HIGHEST)
        for h in range(DN_HEADS):
            q = qkv_ref[rs, h * Dh:(h + 1) * Dh]
            k = qkv_ref[rs, (DN_HEADS + h) * Dh:(DN_HEADS + h + 1) * Dh]
            v = qkv_ref[rs, (2 * DN_HEADS + h) * Dh:(2 * DN_HEADS + h + 1) * Dh]
            beta = beta_all[rs, h:h + 1]
            gcol = gc_c[:, DN_HEADS + h:DN_HEADS + h + 1]
            grow = gc_rows[h:h + 1, :]
            decay = jnp.exp(jnp.where(incl, gcol - grow, -jnp.inf))
            kb = k * beta
            eg = jnp.exp(gcol)
            g_last = gcol[C - 1:C, :]
            qk_kk = _dot_nt_f32(jnp.concatenate([q, kb], axis=0), k)
            m = jnp.where(strict, qk_kk[C:] * decay, 0.0)
            md = jnp.where(same_block, m, 0.0)
            items.append(dict(
                rs=rs, h=h, qk=qk_kk[:C] * decay, md=md, mo=m - md,
                rhs=jnp.concatenate([v * beta, kb * eg], axis=1),
                qe=q * eg, k_dec=k * jnp.exp(g_last - gcol), g_last=jnp.exp(g_last)))

    for it in items:
        it["p"] = -it["md"]
        it["mk"] = _dot_f32(it["md"], it["md"])
    for _ in range(3):
        for it in items:
            both = _dot_f32(jnp.concatenate([it["mk"], it["p"]], axis=0), it["mk"])
            it["p"] = it["p"] + it["mk"] + both[C:]
            it["mk"] = both[:C]
    for it in items:
        it["n"] = it["mo"] + _dot_f32(it["p"], it["mo"])
    for it in items:
        it["n2"] = _dot_f32(it["n"], it["n"])
    for it in items:
        it["q"] = it["n2"] - it["n"] - _dot_f32(it["n2"], it["n"])
    for it in items:
        it["tinv"] = it["q"] + it["p"] + _dot_f32(it["q"], it["p"])
    for it in items:
        it["uw"] = it["rhs"] + _dot_f32(it["tinv"], it["rhs"])
    for it in items:
        ktuw = _dot_tn_f32(it["k_dec"], it["uw"])
        it["ktu"] = ktuw[:, :Dh]
        it["ktw"] = ktuw[:, Dh:]
    for it in items:
        quw = _dot_f32(it["qk"], it["uw"])
        it["o0"] = quw[:, :Dh]
        it["q_eff"] = it["qe"] - quw[:, Dh:]

    states = [state_ref[h] for h in range(DN_HEADS)]
    for c in range(nc):
        chunk_items = items[c * DN_HEADS:(c + 1) * DN_HEADS]
        prods = [_dot_f32(jnp.concatenate([it["ktw"], it["q_eff"]], axis=0), states[it["h"]])
                 for it in chunk_items]
        for it, prod in zip(chunk_items, prods):
            h, rs = it["h"], it["rs"]
            o = prod[Dh:] + it["o0"]
            states[h] = states[h] * it["g_last"] + it["ktu"] - prod[:Dh]
            zg = z_ref[rs, h * Dh:(h + 1) * Dh].astype(F32)
            o_ref[rs, h * Dh:(h + 1) * Dh] = (_rms_norm(o, onw_ref[...]) * _silu(zg)).astype(o_ref.dtype)
    for h in range(DN_HEADS):
        state_ref[h] = states[h]


def _deltanet(dn_qkv, small, z, conv_w, a_log, dt_bias, onorm_w, batch, seq):
    T = dn_qkv.shape[0]
    rows = DN_BLOCK_CHUNKS * DN_CHUNK
    nblocks = seq // rows
    halo = SUBLANES_BF16
    lane = jnp.arange(LANES)
    in_decay_lanes = (lane >= DN_HEADS) & (lane < 2 * DN_HEADS)
    idx = jnp.clip(lane - DN_HEADS, 0, DN_HEADS - 1)
    avec = jnp.where(in_decay_lanes, -jnp.exp(a_log.astype(F32))[idx], 0.0).reshape(1, LANES)
    dtb = jnp.where(in_decay_lanes, dt_bias.astype(F32)[idx], 0.0).reshape(1, LANES)
    row = lambda b, c: (b * nblocks + c, 0)
    halo_map = lambda b, c: (jnp.maximum((b * nblocks + c) * (rows // halo) - 1, 0), 0)
    return pl.pallas_call(
        _deltanet_body,
        grid=(batch, nblocks),
        in_specs=[pl.BlockSpec((halo, 3 * DN_WIDTH), halo_map),
                  pl.BlockSpec((rows, 3 * DN_WIDTH), row),
                  pl.BlockSpec((rows, LANES), row),
                  pl.BlockSpec((rows, DN_WIDTH), row),
                  _resident((DN_CONV, 3 * DN_WIDTH)), _resident((1, LANES)), _resident((1, LANES)),
                  _resident((1, DN_HEAD_DIM))],
        out_specs=pl.BlockSpec((rows, DN_WIDTH), row),
        out_shape=jax.ShapeDtypeStruct((T, DN_WIDTH), BF16),
        scratch_shapes=[pltpu.VMEM((DN_HEADS, DN_HEAD_DIM, DN_HEAD_DIM), F32),
                        pltpu.VMEM((rows, 3 * DN_WIDTH), F32)],
        compiler_params=_params("parallel", "arbitrary"),
        name="gated_deltanet",
    )(dn_qkv, dn_qkv, small, z, conv_w.astype(F32), avec, dtb, onorm_w.astype(F32).reshape(1, DN_HEAD_DIM))


def _merge_body(*refs, dilations):
    ng = len(dilations)
    o_refs = refs[:ng]
    lse_refs = refs[ng:2 * ng]
    ob_ref, gates_ref, x_ref, expand_ref, wpa_ref, wpb_ref, wo_ref, h_ref = refs[2 * ng:2 * ng + 8]
    relay_refs = list(refs[2 * ng + 8:])
    D = x_ref.shape[-1]
    o_nat, lses = [], []
    for o_ref, lse_ref, d in zip(o_refs, lse_refs, dilations):
        if d == 1:
            o_nat.append(o_ref[...].astype(F32))
            lses.append(lse_ref[...])
        else:
            o_nat.append(_from_dilated(o_ref, relay_refs.pop(0), d))
            lses.append(_from_dilated(lse_ref, relay_refs.pop(0), d))
    mx = functools.reduce(jnp.maximum, lses)
    es = [jnp.exp(l - mx) for l in lses]
    inv = 1.0 / functools.reduce(lambda a, b: a + b, es)
    o_a = None
    for e, o_g in zip(es, o_nat):
        alpha = e * inv
        hi = alpha.astype(BF16)
        lo = (alpha - hi.astype(F32)).astype(BF16)
        alpha_wide = jnp.dot(jnp.concatenate([hi, lo], axis=1), expand_ref[...], preferred_element_type=F32)
        term = alpha_wide * o_g
        o_a = term if o_a is None else o_a + term
    pa = jnp.dot(o_a.astype(BF16), wpa_ref[...], preferred_element_type=F32)
    pb = jnp.dot(ob_ref[...], wpb_ref[...], preferred_element_type=F32)
    ga = gates_ref[:, :D].astype(F32)
    gb = gates_ref[:, D:].astype(F32)
    y = _sigmoid(ga) * pa + _sigmoid(gb) * pb
    h_ref[...] = x_ref[...] + jnp.dot(y.astype(BF16), wo_ref[...], preferred_element_type=F32)


def _merge(o_groups, lse_groups, dilations, o_b, gates, x2d, w_pa, w_pb, w_o):
    T, D = x2d.shape
    W = ATTN_GROUP_WIDTH
    lane = jnp.arange(2 * LANES)[:, None] % LANES
    expand = (lane == (jnp.arange(W)[None, :] // ATTN_HEAD_DIM)).astype(BF16)
    row = lambda i: (i, 0)
    tile = lambda n, d=1: pl.BlockSpec((ROW_TILE // d, d * n), row)
    relay = []
    for d in dilations:
        if d > 1:
            relay += [pltpu.VMEM((W // LANES, ROW_TILE, LANES), F32), pltpu.VMEM((1, ROW_TILE, LANES), F32)]
    return pl.pallas_call(
        functools.partial(_merge_body, dilations=tuple(dilations)),
        grid=(T // ROW_TILE,),
        in_specs=[tile(W, d) for d in dilations] + [tile(LANES, d) for d in dilations]
                 + [tile(DN_WIDTH), tile(2 * D), tile(D),
                    _resident(expand.shape), _resident(w_pa.shape), _resident(w_pb.shape), _resident(w_o.shape)],
        out_specs=tile(D),
        out_shape=jax.ShapeDtypeStruct((T, D), F32),
        scratch_shapes=relay,
        compiler_params=_params("parallel"),
        name="merge_out_projection",
    )(*o_groups, *lse_groups, o_b, gates, x2d, expand, w_pa, w_pb, w_o)


def _ffn_body(halo_ref, h_ref, nw_ref, wup_ref, cw_ref, cb_ref, wdown_ref, fw_ref, out_ref, act_ref,
              *, tiles_per_seq, col_tile, final_norm):
    dff = wdown_ref.shape[0]
    halo_rows = halo_ref.shape[0]
    i = pl.program_id(0)
    keep_halo = jnp.where(i % tiles_per_seq > 0, 1.0, 0.0).astype(F32)
    h = h_ref[...]
    hx = jnp.concatenate([halo_ref[...] * keep_halo, h], axis=0)
    hn = _rms_norm(hx, nw_ref[...]).astype(BF16)

    def conv(c0):
        u = jnp.dot(hn, wup_ref[:, c0:c0 + col_tile], preferred_element_type=F32)
        acc = u * cw_ref[FFN_CONV - 1:FFN_CONV, c0:c0 + col_tile]
        for j in range(1, FFN_CONV):
            acc = acc + pltpu.roll(u, j, axis=0) * cw_ref[FFN_CONV - 1 - j:FFN_CONV - j, c0:c0 + col_tile]
        return acc[halo_rows:] + cb_ref[:, c0:c0 + col_tile]

    for c0 in range(0, dff, col_tile):
        act_ref[:, c0:c0 + col_tile] = (_silu(conv(c0)) * conv(dff + c0)).astype(BF16)
    y = h + jnp.dot(act_ref[...], wdown_ref[...], preferred_element_type=F32)
    if final_norm:
        y = _rms_norm(y, fw_ref[...])
    out_ref[...] = y


def _ffn(h2d, seq, norm_w, w_up, conv_w, conv_b, w_down, final_w, final_norm):
    T, D = h2d.shape
    dff = w_down.shape[0]
    halo = SUBLANES_F32
    col_tile = 2 * LANES
    row = lambda i: (i, 0)
    halo_map = lambda i: (jnp.maximum(i * (ROW_TILE // halo) - 1, 0), 0)
    body = functools.partial(_ffn_body, tiles_per_seq=seq // ROW_TILE, col_tile=col_tile, final_norm=final_norm)
    return pl.pallas_call(
        body,
        grid=(T // ROW_TILE,),
        in_specs=[pl.BlockSpec((halo, D), halo_map), pl.BlockSpec((ROW_TILE, D), row), _resident((1, D)),
                  _resident(w_up.shape), _resident((FFN_CONV, 2 * dff)), _resident((1, 2 * dff)),
                  _resident(w_down.shape), _resident((1, D))],
        out_specs=pl.BlockSpec((ROW_TILE, D), row),
        out_shape=jax.ShapeDtypeStruct((T, D), F32),
        scratch_shapes=[pltpu.VMEM((ROW_TILE, dff), BF16)],
        compiler_params=_params("parallel"),
        name="conv_glu_ffn",
    )(h2d, h2d, norm_w.reshape(1, D), w_up, conv_w.astype(F32), conv_b.astype(F32).reshape(1, 2 * dff),
      w_down, final_w.reshape(1, D))


def _pack_in_weights(w_in):
    D = w_in.shape[0]
    ng = len(ATTN_GROUPS)
    aw = ng * ATTN_GROUP_WIDTH
    W = ATTN_GROUP_WIDTH
    cols = []
    for g in range(ng):
        for part in range(3):
            cols.append(w_in[:, part * aw + g * W: part * aw + (g + 1) * W])
    c = 3 * aw
    dn = w_in[:, c:c + 3 * DN_WIDTH]
    c += 3 * DN_WIDTH
    small = w_in[:, c:c + 2 * DN_HEADS]
    c += 2 * DN_HEADS
    z = w_in[:, c:c + DN_WIDTH]
    c += DN_WIDTH
    gates = w_in[:, c:]
    small = jnp.pad(small, ((0, 0), (0, LANES - 2 * DN_HEADS)))
    packed = jnp.concatenate(cols + [dn, z, gates, small], axis=1).astype(BF16)
    widths = [W] * (3 * ng) + [3 * DN_WIDTH, DN_WIDTH, 2 * D, LANES]
    dtypes = [BF16] * (3 * ng + 3) + [F32]
    return packed, widths, dtypes


def kernel(x, norm1_w, w_in, dn_conv_w, dn_a_log, dn_dt_bias, dn_onorm_w, w_pa, w_pb, w_o, norm2_w, w_up,
           ffn_conv_w, ffn_conv_b, w_down, final_norm_w):
    B, S, D = x.shape
    depth = w_in.shape[0]
    assert S % ROW_TILE == 0 and S % (DN_BLOCK_CHUNKS * DN_CHUNK) == 0
    assert all(S % (ATTN_Q_BLOCKS * ATTN_BLOCK * d) == 0 and ROW_TILE % (SUBLANES_BF16 * d) == 0
               for _, d in ATTN_GROUPS)
    xf = x.astype(F32).reshape(B * S, D)
    for l in range(depth):
        packed, widths, dtypes = _pack_in_weights(w_in[l])
        ng = len(ATTN_GROUPS)
        group_dilations = [d for _, d in ATTN_GROUPS]
        out_dilations = [d for d in group_dilations for _ in range(3)] + [1] * (len(widths) - 3 * ng)
        outs = _in_projection(xf, norm1_w[l].astype(F32), packed, widths, dtypes, out_dilations)
        o_groups, lse_groups = [], []
        for g, (window, dilation) in enumerate(ATTN_GROUPS):
            q, k, v = outs[3 * g:3 * g + 3]
            o_g, lse_g = _dilated_attention(q, k, v, B, S, window, dilation)
            o_groups.append(o_g)
            lse_groups.append(lse_g)
        dn_qkv, z, gates, small = outs[3 * ng:]
        o_b = _deltanet(dn_qkv, small, z, dn_conv_w[l], dn_a_log[l], dn_dt_bias[l], dn_onorm_w[l], B, S)
        h = _merge(o_groups, lse_groups, group_dilations, o_b, gates, xf, w_pa[l].astype(BF16),
                   w_pb[l].astype(BF16), w_o[l].astype(BF16))
        xf = _ffn(h, S, norm2_w[l].astype(F32), w_up[l].astype(BF16), ffn_conv_w[l], ffn_conv_b[l],
                  w_down[l].astype(BF16), final_norm_w.astype(F32), final_norm=(l == depth - 1))
    return xf.reshape(B, S, D).astype(x.dtype)
```

```python
import functools

import jax
import jax.numpy as jnp
from jax import lax
from jax.experimental import pallas as pl
from jax.experimental.pallas import tpu as pltpu

F32 = jnp.float32
BF16 = jnp.bfloat16
HIGHEST = lax.Precision.HIGHEST

NORM_EPS = 1e-6

ATTN_HEAD_DIM = 64
ATTN_HEADS_PER_GROUP = 4
ATTN_GROUP_WIDTH = ATTN_HEADS_PER_GROUP * ATTN_HEAD_DIM
ATTN_GROUPS = ((128, 1), (512, 4), (2048, 16))
ATTN_BLOCK = 128
ATTN_STEP_BLOCKS = 4
DN_HEADS = 4
DN_HEAD_DIM = 128
DN_WIDTH = DN_HEADS * DN_HEAD_DIM
DN_CONV = 4
DN_CHUNK = 64
DN_BLOCK_CHUNKS = 4
DN_SOLVE_BLOCK = 16
FFN_CONV = 3

LANES = 128
SUBLANES_F32 = 8
SUBLANES_BF16 = 16
VMEM_LIMIT_BYTES = 56 * 1024 * 1024

ROW_TILE = 512


def _resident(shape):
    nd = len(shape)
    return pl.BlockSpec(shape, lambda *_: (0,) * nd, pipeline_mode=pl.Buffered(1))


def _params(*semantics):
    return pltpu.CompilerParams(dimension_semantics=semantics, vmem_limit_bytes=VMEM_LIMIT_BYTES)


def _rms_norm(x, w):
    return x * lax.rsqrt(jnp.mean(x * x, axis=-1, keepdims=True) + NORM_EPS) * w


def _silu(x):
    return x * (1.0 / (1.0 + jnp.exp(-x)))


def _sigmoid(x):
    return 1.0 / (1.0 + jnp.exp(-x))


def _to_dilated(y, relay_ref, o_ref, d):
    rows = y.shape[0]
    for half in range(ATTN_GROUP_WIDTH // LANES):
        relay_ref[half] = y[:, half * LANES:(half + 1) * LANES]
    for r in range(d):
        for half in range(ATTN_GROUP_WIDTH // LANES):
            c0 = r * ATTN_GROUP_WIDTH + half * LANES
            o_ref[:, c0:c0 + LANES] = relay_ref[half, pl.ds(r, rows // d, stride=d), :].astype(o_ref.dtype)


def _from_dilated(x_ref, relay_ref, d):
    rows = relay_ref.shape[1]
    nslab = relay_ref.shape[0]
    width = nslab * LANES
    for r in range(d):
        for slab in range(nslab):
            c0 = r * width + slab * LANES
            relay_ref[slab, pl.ds(r, rows // d, stride=d), :] = x_ref[:, c0:c0 + LANES].astype(F32)
    return jnp.concatenate([relay_ref[slab] for slab in range(nslab)], axis=1)


def _inproj_body(x_ref, nw_ref, w_ref, cw_ref, *refs, dilations, dn_index, tiles_per_seq):
    n_out = len(dilations)
    out_refs = refs[:n_out]
    ydn_ref = refs[n_out]
    relay_refs = refs[n_out + 1:]
    i = pl.program_id(0)
    xn = _rms_norm(x_ref[...], nw_ref[...]).astype(BF16)
    offsets = []
    c0 = 0
    for o_ref, d in zip(out_refs, dilations):
        offsets.append(c0)
        c0 += o_ref.shape[-1] // d

    halo = ydn_ref.shape[0] - x_ref.shape[0]
    rows = x_ref.shape[0]
    dn_ref = out_refs[dn_index]

    @pl.when(i % tiles_per_seq == 0)
    def _():
        ydn_ref[0:halo, :] = jnp.zeros((halo, ydn_ref.shape[1]), F32)

    @pl.when(i % tiles_per_seq > 0)
    def _():
        ydn_ref[0:halo, :] = ydn_ref[rows:rows + halo, :]

    step = 2 * DN_HEAD_DIM
    for cb in range(0, 3 * DN_WIDTH, step):
        ydn_ref[halo:, cb:cb + step] = jnp.dot(xn, w_ref[:, offsets[dn_index] + cb:offsets[dn_index] + cb + step],
                                               preferred_element_type=F32)
    conv_blocks = list(range(0, 3 * DN_WIDTH, DN_HEAD_DIM))

    n_relay = 0
    for j, (o_ref, d) in enumerate(zip(out_refs, dilations)):
        if j == dn_index:
            continue
        n = o_ref.shape[-1] // d
        y = jnp.dot(xn, w_ref[:, offsets[j]:offsets[j] + n], preferred_element_type=F32)
        if d == 1:
            o_ref[...] = y.astype(o_ref.dtype)
        else:
            _to_dilated(y, relay_refs[n_relay], o_ref, d)
            n_relay += 1
        if conv_blocks:
            _dn_conv_block(ydn_ref, cw_ref, dn_ref, conv_blocks.pop(0), halo)
    while conv_blocks:
        _dn_conv_block(ydn_ref, cw_ref, dn_ref, conv_blocks.pop(0), halo)


def _dn_conv_block(ydn_ref, cw_ref, o_ref, col, halo):
    Dh = DN_HEAD_DIM
    rows = o_ref.shape[0]
    lanes = slice(col, col + Dh)
    acc = ydn_ref[halo:, lanes] * cw_ref[DN_CONV - 1:DN_CONV, lanes]
    for j in range(1, DN_CONV):
        acc = acc + ydn_ref[halo - j:halo - j + rows, lanes] * cw_ref[DN_CONV - 1 - j:DN_CONV - j, lanes]
    a = _silu(acc)
    if col < 2 * DN_WIDTH:
        a = a * lax.rsqrt(jnp.sum(a * a, axis=-1, keepdims=True) + NORM_EPS)
    if col < DN_WIDTH:
        a = a * (Dh ** -0.5)
    o_ref[:, lanes] = a.astype(o_ref.dtype)


def _in_projection(x2d, norm_w, w_packed, dn_conv_w, out_widths, out_dtypes, dilations, dn_index, seq):
    T, D = x2d.shape
    grid = (T // ROW_TILE,)
    row = lambda i: (i, 0)
    n_relay = sum(1 for d in dilations if d > 1)
    body = functools.partial(_inproj_body, dilations=tuple(dilations), dn_index=dn_index,
                             tiles_per_seq=seq // ROW_TILE)
    return pl.pallas_call(
        body,
        grid=grid,
        in_specs=[pl.BlockSpec((ROW_TILE, D), row), _resident((1, D)), _resident(w_packed.shape),
                  _resident(dn_conv_w.shape)],
        out_specs=[pl.BlockSpec((ROW_TILE // d, d * n), row) for n, d in zip(out_widths, dilations)],
        out_shape=[jax.ShapeDtypeStruct((T // d, d * n), dt)
                   for n, dt, d in zip(out_widths, out_dtypes, dilations)],
        scratch_shapes=[pltpu.VMEM((SUBLANES_F32 + ROW_TILE, 3 * DN_WIDTH), F32)]
                       + [pltpu.VMEM((ATTN_GROUP_WIDTH // LANES, ROW_TILE, LANES), F32)] * n_relay,
        compiler_params=_params("arbitrary"),
        name="in_projection",
    )(x2d, norm_w.reshape(1, D), w_packed, dn_conv_w.astype(F32))


def _attn_body(q_ref, kp_ref, kc_ref, vp_ref, vc_ref, o_ref, lse_ref, *, span):
    blk = ATTN_BLOCK
    W = ATTN_GROUP_WIDTH
    H = ATTN_HEADS_PER_GROUP
    nq = q_ref.shape[0] // blk
    nr = q_ref.shape[1] // W
    i = pl.program_id(2)
    row = lax.broadcasted_iota(jnp.int32, (blk, 2 * blk), 0)
    col = lax.broadcasted_iota(jnp.int32, (blk, 2 * blk), 1)
    rel = row + blk - col
    band = (rel >= 0) & (rel <= span)
    first_key = jnp.where(i > 0, 0, blk)
    band_first = band & (col >= first_key)
    lane_head = lax.broadcasted_iota(jnp.int32, (blk, W), 1) // ATTN_HEAD_DIM
    lse_lane = lax.broadcasted_iota(jnp.int32, (blk, LANES), 1)
    units = [(rr, jb) for rr in range(nr) for jb in range(nq)]

    scores = []
    for rr, jb in units:
        lanes = slice(rr * W, (rr + 1) * W)
        q = q_ref[jb * blk:(jb + 1) * blk, lanes] * (ATTN_HEAD_DIM ** -0.5)
        qs = jnp.concatenate([jnp.where(lane_head == h, q, jnp.zeros_like(q)) for h in range(H)], axis=0)
        kwin = (jnp.concatenate([kp_ref[:, lanes], kc_ref[0:blk, lanes]], axis=0) if jb == 0
                else kc_ref[(jb - 1) * blk:(jb + 1) * blk, lanes])
        scores.append(lax.dot_general(qs, kwin, (((1,), (1,)), ((), ())), preferred_element_type=F32))
    probs, maxes, dens = [], [], []
    for (rr, jb), sc in zip(units, scores):
        valid = band_first if jb == 0 else band
        for h in range(H):
            s = jnp.where(valid, sc[h * blk:(h + 1) * blk], -jnp.inf)
            m = jnp.max(s, axis=-1, keepdims=True)
            p = jnp.exp(s - m)
            probs.append(p.astype(BF16))
            maxes.append(m)
            dens.append(jnp.sum(p, axis=-1, keepdims=True))
    for u, (rr, jb) in enumerate(units):
        lanes = slice(rr * W, (rr + 1) * W)
        vwin = (jnp.concatenate([vp_ref[:, lanes], vc_ref[0:blk, lanes]], axis=0) if jb == 0
                else vc_ref[(jb - 1) * blk:(jb + 1) * blk, lanes])
        pv = jnp.dot(jnp.concatenate(probs[u * H:(u + 1) * H], axis=0), vwin,
                     preferred_element_type=F32)
        o_acc = jnp.zeros((blk, W), F32)
        lse_acc = jnp.zeros((blk, LANES), F32)
        for h in range(H):
            den = dens[u * H + h]
            o_acc = jnp.where(lane_head == h, pv[h * blk:(h + 1) * blk] / den, o_acc)
            lse_acc = jnp.where(lse_lane == h, maxes[u * H + h] + jnp.log(den), lse_acc)
        o_ref[jb * blk:(jb + 1) * blk, lanes] = o_acc.astype(o_ref.dtype)
        lse_ref[jb * blk:(jb + 1) * blk, rr * LANES:(rr + 1) * LANES] = lse_acc


def _dilated_attention(q, k, v, batch, seq, window, dilation):
    d = dilation
    L = seq // d
    W = ATTN_GROUP_WIDTH
    nq = min(ATTN_STEP_BLOCKS, L // ATTN_BLOCK)
    nr = min(ATTN_STEP_BLOCKS // nq, d)
    qrows = nq * ATTN_BLOCK
    assert L % qrows == 0 and d % nr == 0
    view = lambda t: t.reshape(batch, L, t.shape[-1])
    cur = lambda b, r, i: (b, i, r)
    prev = lambda b, r, i: (b, jnp.maximum(i * nq - 1, 0), r)
    cur_spec = pl.BlockSpec((None, qrows, nr * W), cur)
    prev_spec = pl.BlockSpec((None, ATTN_BLOCK, nr * W), prev)
    o, lse = pl.pallas_call(
        functools.partial(_attn_body, span=window // d),
        grid=(batch, d // nr, L // qrows),
        in_specs=[cur_spec, prev_spec, cur_spec, prev_spec, cur_spec],
        out_specs=[cur_spec, pl.BlockSpec((None, qrows, nr * LANES), cur)],
        out_shape=[jax.ShapeDtypeStruct((batch, L, d * W), BF16),
                   jax.ShapeDtypeStruct((batch, L, d * LANES), F32)],
        compiler_params=_params("parallel", "parallel", "parallel"),
        name=f"dilated_attention_d{d}",
    )(view(q), view(k), view(k), view(v), view(v))
    return o.reshape(batch * L, d * W), lse.reshape(batch * L, d * LANES)


def _mm(a, b):
    return jnp.dot(a, b, preferred_element_type=F32)


def _mm_nt(a, b):
    return lax.dot_general(a, b, (((1,), (1,)), ((), ())), preferred_element_type=F32)


def _mm_tn(a, b):
    return lax.dot_general(a, b, (((0,), (0,)), ((), ())), preferred_element_type=F32)


def _deltanet_body(x_ref, sm_ref, z_ref, avec_ref, dtb_ref, onw_ref, o_ref, state_ref):
    C = DN_CHUNK
    Dh = DN_HEAD_DIM
    rows = x_ref.shape[0]
    blk = pl.program_id(1)

    @pl.when(blk == 0)
    def _():
        state_ref[...] = jnp.zeros_like(state_ref)

    sm = sm_ref[...]
    beta_all = _sigmoid(sm)
    sp_in = sm + dtb_ref[...]
    softplus = jnp.maximum(sp_in, 0.0) + jnp.log(1.0 + jnp.exp(-jnp.abs(sp_in)))
    g_all = avec_ref[...] * softplus
    row_in_chunk = lax.broadcasted_iota(jnp.int32, (rows, LANES), 0) % C
    gc_all = g_all
    s = 1
    while s < C:
        gc_all = gc_all + jnp.where(row_in_chunk >= s, pltpu.roll(gc_all, s, axis=0), 0.0)
        s *= 2
    sel = jnp.where(lax.broadcasted_iota(jnp.int32, (SUBLANES_F32, LANES), 1)
                    == lax.broadcasted_iota(jnp.int32, (SUBLANES_F32, LANES), 0) + DN_HEADS, 1.0, 0.0).astype(F32)

    ri = lax.broadcasted_iota(jnp.int32, (C, C), 0)
    ci = lax.broadcasted_iota(jnp.int32, (C, C), 1)
    incl = ri >= ci
    strict = ri > ci
    same_block = (ri // DN_SOLVE_BLOCK) == (ci // DN_SOLVE_BLOCK)
    nc = rows // C

    items = []
    for c in range(nc):
        rs = slice(c * C, (c + 1) * C)
        gc_c = gc_all[rs]
        gc_rows = lax.dot_general(sel, gc_c, (((1,), (1,)), ((), ())), preferred_element_type=F32,
                                  precision=HIGHEST)
        for h in range(DN_HEADS):
            q_b = x_ref[rs, h * Dh:(h + 1) * Dh]
            k_b = x_ref[rs, (DN_HEADS + h) * Dh:(DN_HEADS + h + 1) * Dh]
            k = k_b.astype(F32)
            v = x_ref[rs, (2 * DN_HEADS + h) * Dh:(2 * DN_HEADS + h + 1) * Dh].astype(F32)
            beta = beta_all[rs, h:h + 1]
            gcol = gc_c[:, DN_HEADS + h:DN_HEADS + h + 1]
            grow = gc_rows[h:h + 1, :]
            decay = jnp.exp(jnp.where(incl, gcol - grow, -jnp.inf))
            kb = k * beta
            eg = jnp.exp(gcol)
            g_last = gcol[C - 1:C, :]
            qk_kk = _mm_nt(jnp.concatenate([q_b, kb.astype(BF16)], axis=0), k_b)
            m = jnp.where(strict, qk_kk[C:] * decay, 0.0)
            md = jnp.where(same_block, m, 0.0)
            items.append(dict(
                rs=rs, h=h, qk=(qk_kk[:C] * decay).astype(BF16), md=md, mo=m - md,
                rhs=jnp.concatenate([v * beta, kb * eg], axis=1),
                qe=q_b.astype(F32) * eg, k_dec=(k * jnp.exp(g_last - gcol)).astype(BF16),
                g_last=jnp.exp(g_last)))

    for it in items:
        md_b = it["md"].astype(BF16)
        it["p"] = -it["md"]
        it["mk"] = _mm(md_b, md_b)
    for step in range(3):
        for it in items:
            mk_b = it["mk"].astype(BF16)
            p_b = it["p"].astype(BF16)
            if step < 2:
                both = _mm(jnp.concatenate([mk_b, p_b], axis=0), mk_b)
                it["p"] = it["p"] + it["mk"] + both[C:]
                it["mk"] = both[:C]
            else:
                it["p"] = it["p"] + it["mk"] + _mm(p_b, mk_b)
    for it in items:
        it["p_b"] = it["p"].astype(BF16)
        it["n"] = it["mo"] + _mm(it["p_b"], it["mo"].astype(BF16))
    for it in items:
        it["n_b"] = it["n"].astype(BF16)
        it["n2"] = _mm(it["n_b"], it["n_b"])
    for it in items:
        it["q"] = it["n2"] - it["n"] - _mm(it["n2"].astype(BF16), it["n_b"])
    for it in items:
        it["tinv"] = it["q"] + it["p"] + _mm(it["q"].astype(BF16), it["p_b"])
    for it in items:
        uw = it["rhs"] + _mm(it["tinv"].astype(BF16), it["rhs"].astype(BF16))
        it["uw_b"] = uw.astype(BF16)
    for it in items:
        ktuw = _mm_tn(it["k_dec"], it["uw_b"])
        it["ktu"] = ktuw[:, :Dh]
        it["ktw_b"] = ktuw[:, Dh:].astype(BF16)
    for it in items:
        quw = _mm(it["qk"], it["uw_b"])
        it["o0"] = quw[:, :Dh]
        it["q_eff_b"] = (it["qe"] - quw[:, Dh:]).astype(BF16)

    states = [state_ref[h] for h in range(DN_HEADS)]
    for c in range(nc):
        chunk_items = items[c * DN_HEADS:(c + 1) * DN_HEADS]
        prods = [_mm(jnp.concatenate([it["ktw_b"], it["q_eff_b"]], axis=0), states[it["h"]].astype(BF16))
                 for it in chunk_items]
        for it, prod in zip(chunk_items, prods):
            h, rs = it["h"], it["rs"]
            o = prod[Dh:] + it["o0"]
            states[h] = states[h] * it["g_last"] + it["ktu"] - prod[:Dh]
            zg = z_ref[rs, h * Dh:(h + 1) * Dh].astype(F32)
            o_ref[rs, h * Dh:(h + 1) * Dh] = (_rms_norm(o, onw_ref[...]) * _silu(zg)).astype(o_ref.dtype)
    for h in range(DN_HEADS):
        state_ref[h] = states[h]


def _deltanet(dn_qkv, small, z, a_log, dt_bias, onorm_w, batch, seq):
    T = dn_qkv.shape[0]
    rows = DN_BLOCK_CHUNKS * DN_CHUNK
    nblocks = seq // rows
    lane = jnp.arange(LANES)
    in_decay_lanes = (lane >= DN_HEADS) & (lane < 2 * DN_HEADS)
    idx = jnp.clip(lane - DN_HEADS, 0, DN_HEADS - 1)
    avec = jnp.where(in_decay_lanes, -jnp.exp(a_log.astype(F32))[idx], 0.0).reshape(1, LANES)
    dtb = jnp.where(in_decay_lanes, dt_bias.astype(F32)[idx], 0.0).reshape(1, LANES)
    row = lambda b, c: (b * nblocks + c, 0)
    return pl.pallas_call(
        _deltanet_body,
        grid=(batch, nblocks),
        in_specs=[pl.BlockSpec((rows, 3 * DN_WIDTH), row),
                  pl.BlockSpec((rows, LANES), row),
                  pl.BlockSpec((rows, DN_WIDTH), row),
                  _resident((1, LANES)), _resident((1, LANES)), _resident((1, DN_HEAD_DIM))],
        out_specs=pl.BlockSpec((rows, DN_WIDTH), row),
        out_shape=jax.ShapeDtypeStruct((T, DN_WIDTH), BF16),
        scratch_shapes=[pltpu.VMEM((DN_HEADS, DN_HEAD_DIM, DN_HEAD_DIM), F32)],
        compiler_params=_params("parallel", "arbitrary"),
        name="gated_deltanet",
    )(dn_qkv, small, z, avec, dtb, onorm_w.astype(F32).reshape(1, DN_HEAD_DIM))


def _mixer_output(o_refs, lse_refs, relay_refs, dilations, ob_ref, gates_ref, x_ref, expand_ref, wpa_ref, wpb_ref,
                  wo_ref):
    D = x_ref.shape[-1]
    relay_refs = list(relay_refs)
    o_nat, lses = [], []
    for o_ref, lse_ref, d in zip(o_refs, lse_refs, dilations):
        if d == 1:
            o_nat.append(o_ref[...].astype(F32))
            lses.append(lse_ref[...])
        else:
            o_nat.append(_from_dilated(o_ref, relay_refs.pop(0), d))
            lses.append(_from_dilated(lse_ref, relay_refs.pop(0), d))
    mx = functools.reduce(jnp.maximum, lses)
    es = [jnp.exp(l - mx) for l in lses]
    inv = 1.0 / functools.reduce(lambda a, b: a + b, es)
    o_a = None
    for e, o_g in zip(es, o_nat):
        alpha = e * inv
        hi = alpha.astype(BF16)
        lo = (alpha - hi.astype(F32)).astype(BF16)
        alpha_wide = jnp.dot(jnp.concatenate([hi, lo], axis=1), expand_ref[...], preferred_element_type=F32)
        term = alpha_wide * o_g
        o_a = term if o_a is None else o_a + term
    pa = jnp.dot(o_a.astype(BF16), wpa_ref[...], preferred_element_type=F32)
    pb = jnp.dot(ob_ref[...], wpb_ref[...], preferred_element_type=F32)
    ga = gates_ref[:, :D].astype(F32)
    gb = gates_ref[:, D:].astype(F32)
    y = _sigmoid(ga) * pa + _sigmoid(gb) * pb
    return x_ref[...] + jnp.dot(y.astype(BF16), wo_ref[...], preferred_element_type=F32)


def _mixer_ffn_body(*refs, dilations, tiles_per_seq, col_tile, final_norm):
    ng = len(dilations)
    o_refs = refs[:ng]
    lse_refs = refs[ng:2 * ng]
    (ob_ref, gates_ref, x_ref, expand_ref, wpa_ref, wpb_ref, wo_ref,
     nw_ref, wup_ref, cw_ref, cb_ref, wdown_ref, fw_ref, out_ref, act_ref, hcarry_ref) = refs[2 * ng:2 * ng + 16]
    relay_refs = refs[2 * ng + 16:]
    dff = wdown_ref.shape[0]
    halo_rows = hcarry_ref.shape[0]
    rows = x_ref.shape[0]
    i = pl.program_id(0)

    @pl.when(i % tiles_per_seq == 0)
    def _():
        hcarry_ref[...] = jnp.zeros_like(hcarry_ref)

    h = _mixer_output(o_refs, lse_refs, relay_refs, dilations, ob_ref, gates_ref, x_ref, expand_ref, wpa_ref,
                      wpb_ref, wo_ref)
    left = hcarry_ref[...]
    hcarry_ref[...] = h[rows - halo_rows:]
    hn = _rms_norm(jnp.concatenate([left, h], axis=0), nw_ref[...]).astype(BF16)

    def conv(c0):
        u = jnp.dot(hn, wup_ref[:, c0:c0 + col_tile], preferred_element_type=F32)
        acc = u * cw_ref[FFN_CONV - 1:FFN_CONV, c0:c0 + col_tile]
        for j in range(1, FFN_CONV):
            acc = acc + pltpu.roll(u, j, axis=0) * cw_ref[FFN_CONV - 1 - j:FFN_CONV - j, c0:c0 + col_tile]
        return acc[halo_rows:] + cb_ref[:, c0:c0 + col_tile]

    for c0 in range(0, dff, col_tile):
        act_ref[:, c0:c0 + col_tile] = (_silu(conv(c0)) * conv(dff + c0)).astype(BF16)
    y = h + jnp.dot(act_ref[...], wdown_ref[...], preferred_element_type=F32)
    if final_norm:
        y = _rms_norm(y, fw_ref[...])
    out_ref[...] = y


def _mixer_ffn(o_groups, lse_groups, dilations, o_b, gates, x2d, w_pa, w_pb, w_o, seq, norm_w, w_up, conv_w, conv_b,
               w_down, final_w, final_norm):
    T, D = x2d.shape
    W = ATTN_GROUP_WIDTH
    dff = w_down.shape[0]
    col_tile = 2 * LANES
    lane = jnp.arange(2 * LANES)[:, None] % LANES
    expand = (lane == (jnp.arange(W)[None, :] // ATTN_HEAD_DIM)).astype(BF16)
    row = lambda i: (i, 0)
    tile = lambda n, d=1: pl.BlockSpec((ROW_TILE // d, d * n), row)
    relay = []
    for d in dilations:
        if d > 1:
            relay += [pltpu.VMEM((W // LANES, ROW_TILE, LANES), F32), pltpu.VMEM((1, ROW_TILE, LANES), F32)]
    body = functools.partial(_mixer_ffn_body, dilations=tuple(dilations), tiles_per_seq=seq // ROW_TILE,
                             col_tile=col_tile, final_norm=final_norm)
    return pl.pallas_call(
        body,
        grid=(T // ROW_TILE,),
        in_specs=[tile(W, d) for d in dilations] + [tile(LANES, d) for d in dilations]
                 + [tile(DN_WIDTH), tile(2 * D), tile(D),
                    _resident(expand.shape), _resident(w_pa.shape), _resident(w_pb.shape), _resident(w_o.shape),
                    _resident((1, D)), _resident(w_up.shape), _resident((FFN_CONV, 2 * dff)),
                    _resident((1, 2 * dff)), _resident(w_down.shape), _resident((1, D))],
        out_specs=tile(D),
        out_shape=jax.ShapeDtypeStruct((T, D), F32),
        scratch_shapes=[pltpu.VMEM((ROW_TILE, dff), BF16), pltpu.VMEM((SUBLANES_F32, D), F32)] + relay,
        compiler_params=_params("arbitrary"),
        name="mixer_out_conv_glu_ffn",
    )(*o_groups, *lse_groups, o_b, gates, x2d, expand, w_pa, w_pb, w_o,
      norm_w.reshape(1, D), w_up, conv_w.astype(F32), conv_b.astype(F32).reshape(1, 2 * dff), w_down,
      final_w.reshape(1, D))


def _pack_in_weights(w_in):
    D = w_in.shape[0]
    ng = len(ATTN_GROUPS)
    aw = ng * ATTN_GROUP_WIDTH
    W = ATTN_GROUP_WIDTH
    cols = []
    for g in range(ng):
        for part in range(3):
            cols.append(w_in[:, part * aw + g * W: part * aw + (g + 1) * W])
    c = 3 * aw
    dn = w_in[:, c:c + 3 * DN_WIDTH]
    c += 3 * DN_WIDTH
    small = w_in[:, c:c + 2 * DN_HEADS]
    c += 2 * DN_HEADS
    z = w_in[:, c:c + DN_WIDTH]
    c += DN_WIDTH
    gates = w_in[:, c:]
    small = jnp.pad(small, ((0, 0), (0, LANES - 2 * DN_HEADS)))
    packed = jnp.concatenate(cols + [dn, z, gates, small], axis=1).astype(BF16)
    widths = [W] * (3 * ng) + [3 * DN_WIDTH, DN_WIDTH, 2 * D, LANES]
    dtypes = [BF16] * (3 * ng + 3) + [F32]
    return packed, widths, dtypes


def kernel(x, norm1_w, w_in, dn_conv_w, dn_a_log, dn_dt_bias, dn_onorm_w, w_pa, w_pb, w_o, norm2_w, w_up,
           ffn_conv_w, ffn_conv_b, w_down, final_norm_w):
    B, S, D = x.shape
    depth = w_in.shape[0]
    assert S % ROW_TILE == 0 and S % (DN_BLOCK_CHUNKS * DN_CHUNK) == 0
    assert all(ROW_TILE % (SUBLANES_BF16 * d) == 0 for _, d in ATTN_GROUPS)
    xf = x.astype(F32).reshape(B * S, D)
    for l in range(depth):
        packed, widths, dtypes = _pack_in_weights(w_in[l])
        ng = len(ATTN_GROUPS)
        group_dilations = [d for _, d in ATTN_GROUPS]
        out_dilations = [d for d in group_dilations for _ in range(3)] + [1] * (len(widths) - 3 * ng)
        outs = _in_projection(xf, norm1_w[l].astype(F32), packed, dn_conv_w[l], widths, dtypes, out_dilations,
                              dn_index=3 * ng, seq=S)
        o_groups, lse_groups = [], []
        for g, (window, dilation) in enumerate(ATTN_GROUPS):
            q, k, v = outs[3 * g:3 * g + 3]
            o_g, lse_g = _dilated_attention(q, k, v, B, S, window, dilation)
            o_groups.append(o_g)
            lse_groups.append(lse_g)
        dn_qkv, z, gates, small = outs[3 * ng:]
        o_b = _deltanet(dn_qkv, small, z, dn_a_log[l], dn_dt_bias[l], dn_onorm_w[l], B, S)
        xf = _mixer_ffn(o_groups, lse_groups, group_dilations, o_b, gates, xf, w_pa[l].astype(BF16),
                        w_pb[l].astype(BF16), w_o[l].astype(BF16), S, norm2_w[l].astype(F32), w_up[l].astype(BF16),
                        ffn_conv_w[l], ffn_conv_b[l], w_down[l].astype(BF16), final_norm_w.astype(F32),
                        final_norm=(l == depth - 1))
    return xf.reshape(B, S, D).astype(x.dtype)
```

```python
import functools

import jax
import jax.numpy as jnp
from jax import lax
from jax.experimental import pallas as pl
from jax.experimental.pallas import tpu as pltpu

F32 = jnp.float32
BF16 = jnp.bfloat16
HIGHEST = lax.Precision.HIGHEST

NORM_EPS = 1e-6

ATTN_HEAD_DIM = 64
ATTN_HEADS_PER_GROUP = 4
ATTN_GROUP_WIDTH = ATTN_HEADS_PER_GROUP * ATTN_HEAD_DIM
ATTN_GROUPS = ((128, 1), (512, 4), (2048, 16))
ATTN_BLOCK = 128
ATTN_STEP_BLOCKS = 4
DN_HEADS = 4
DN_HEAD_DIM = 128
DN_WIDTH = DN_HEADS * DN_HEAD_DIM
DN_CONV = 4
DN_CHUNK = 64
DN_BLOCK_CHUNKS = 8
DN_SOLVE_BLOCK = 16
FFN_CONV = 3

LANES = 128
SUBLANES_F32 = 8
SUBLANES_BF16 = 16
VMEM_LIMIT_BYTES = 56 * 1024 * 1024

ROW_TILE = 512


def _resident(shape):
    nd = len(shape)
    return pl.BlockSpec(shape, lambda *_: (0,) * nd, pipeline_mode=pl.Buffered(1))


def _layer_resident(stacked, layer):
    nd = stacked.ndim - 1
    return pl.BlockSpec((None,) + stacked.shape[1:], lambda *_: (layer,) + (0,) * nd, pipeline_mode=pl.Buffered(1))


def _params(*semantics):
    return pltpu.CompilerParams(dimension_semantics=semantics, vmem_limit_bytes=VMEM_LIMIT_BYTES)


def _rms_norm(x, w):
    return x * lax.rsqrt(jnp.mean(x * x, axis=-1, keepdims=True) + NORM_EPS) * w


def _silu(x):
    return x * (1.0 / (1.0 + jnp.exp(-x)))


def _sigmoid(x):
    return 1.0 / (1.0 + jnp.exp(-x))


def _to_dilated(y, relay_ref, o_ref, d):
    rows = y.shape[0]
    for half in range(ATTN_GROUP_WIDTH // LANES):
        relay_ref[half] = y[:, half * LANES:(half + 1) * LANES]
    for r in range(d):
        for half in range(ATTN_GROUP_WIDTH // LANES):
            c0 = r * ATTN_GROUP_WIDTH + half * LANES
            o_ref[:, c0:c0 + LANES] = relay_ref[half, pl.ds(r, rows // d, stride=d), :].astype(o_ref.dtype)


def _from_dilated(x_ref, relay_ref, d):
    rows = relay_ref.shape[1]
    nslab = relay_ref.shape[0]
    width = nslab * LANES
    for r in range(d):
        for slab in range(nslab):
            c0 = r * width + slab * LANES
            relay_ref[slab, pl.ds(r, rows // d, stride=d), :] = x_ref[:, c0:c0 + LANES].astype(F32)
    return jnp.concatenate([relay_ref[slab] for slab in range(nslab)], axis=1)


def _inproj_body(x_ref, nw_ref, wmain_ref, wtail_ref, cw_ref, *refs, dilations, columns, dn_index, tiles_per_seq):
    n_out = len(dilations)
    out_refs = refs[:n_out]
    ydn_ref = refs[n_out]
    relay_refs = refs[n_out + 1:]
    w_refs = (wmain_ref, wtail_ref)
    i = pl.program_id(0)
    xn = _rms_norm(x_ref[...], nw_ref[...]).astype(BF16)

    halo = ydn_ref.shape[0] - x_ref.shape[0]
    rows = x_ref.shape[0]
    dn_ref = out_refs[dn_index]

    @pl.when(i % tiles_per_seq == 0)
    def _():
        ydn_ref[0:halo, :] = jnp.zeros((halo, ydn_ref.shape[1]), F32)

    @pl.when(i % tiles_per_seq > 0)
    def _():
        ydn_ref[0:halo, :] = ydn_ref[rows:rows + halo, :]

    step = 2 * DN_HEAD_DIM
    dn_w, dn_c0 = columns[dn_index]
    for cb in range(0, 3 * DN_WIDTH, step):
        ydn_ref[halo:, cb:cb + step] = jnp.dot(xn, w_refs[dn_w][:, dn_c0 + cb:dn_c0 + cb + step],
                                               preferred_element_type=F32)
    conv_blocks = list(range(0, 3 * DN_WIDTH, DN_HEAD_DIM))

    n_relay = 0
    for j, (o_ref, d) in enumerate(zip(out_refs, dilations)):
        if j == dn_index:
            continue
        n = o_ref.shape[-1] // d
        wi, c0 = columns[j]
        y = jnp.dot(xn, w_refs[wi][:, c0:c0 + n], preferred_element_type=F32)
        if d == 1:
            o_ref[...] = y.astype(o_ref.dtype)
        else:
            _to_dilated(y, relay_refs[n_relay], o_ref, d)
            n_relay += 1
        if conv_blocks:
            _dn_conv_block(ydn_ref, cw_ref, dn_ref, conv_blocks.pop(0), halo)
    while conv_blocks:
        _dn_conv_block(ydn_ref, cw_ref, dn_ref, conv_blocks.pop(0), halo)


def _dn_conv_block(ydn_ref, cw_ref, o_ref, col, halo):
    Dh = DN_HEAD_DIM
    rows = o_ref.shape[0]
    lanes = slice(col, col + Dh)
    acc = ydn_ref[halo:, lanes] * cw_ref[DN_CONV - 1:DN_CONV, lanes]
    for j in range(1, DN_CONV):
        acc = acc + ydn_ref[halo - j:halo - j + rows, lanes] * cw_ref[DN_CONV - 1 - j:DN_CONV - j, lanes]
    a = _silu(acc)
    if col < 2 * DN_WIDTH:
        a = a * lax.rsqrt(jnp.sum(a * a, axis=-1, keepdims=True) + NORM_EPS)
    if col < DN_WIDTH:
        a = a * (Dh ** -0.5)
    o_ref[:, lanes] = a.astype(o_ref.dtype)


def _in_projection(x2d, layer, norm_w, w_main, w_tail, dn_conv_w, out_widths, out_dtypes, dilations, columns,
                   dn_index, seq):
    T, D = x2d.shape
    grid = (T // ROW_TILE,)
    row = lambda i: (i, 0)
    n_relay = sum(1 for d in dilations if d > 1)
    body = functools.partial(_inproj_body, dilations=tuple(dilations), columns=tuple(columns), dn_index=dn_index,
                             tiles_per_seq=seq // ROW_TILE)
    return pl.pallas_call(
        body,
        grid=grid,
        in_specs=[pl.BlockSpec((ROW_TILE, D), row), _layer_resident(norm_w, layer), _layer_resident(w_main, layer),
                  _layer_resident(w_tail, layer), _layer_resident(dn_conv_w, layer)],
        out_specs=[pl.BlockSpec((ROW_TILE // d, d * n), row) for n, d in zip(out_widths, dilations)],
        out_shape=[jax.ShapeDtypeStruct((T // d, d * n), dt)
                   for n, dt, d in zip(out_widths, out_dtypes, dilations)],
        scratch_shapes=[pltpu.VMEM((SUBLANES_F32 + ROW_TILE, 3 * DN_WIDTH), F32)]
                       + [pltpu.VMEM((ATTN_GROUP_WIDTH // LANES, ROW_TILE, LANES), F32)] * n_relay,
        compiler_params=_params("arbitrary"),
        name="in_projection",
    )(x2d, norm_w, w_main, w_tail, dn_conv_w)


def _attn_body(q_ref, kp_ref, kc_ref, vp_ref, vc_ref, o_ref, lse_ref, *, span):
    blk = ATTN_BLOCK
    W = ATTN_GROUP_WIDTH
    H = ATTN_HEADS_PER_GROUP
    nq = q_ref.shape[0] // blk
    nr = q_ref.shape[1] // W
    i = pl.program_id(2)
    row = lax.broadcasted_iota(jnp.int32, (blk, 2 * blk), 0)
    col = lax.broadcasted_iota(jnp.int32, (blk, 2 * blk), 1)
    rel = row + blk - col
    band = (rel >= 0) & (rel <= span)
    first_key = jnp.where(i > 0, 0, blk)
    band_first = band & (col >= first_key)
    lane_head = lax.broadcasted_iota(jnp.int32, (blk, W), 1) // ATTN_HEAD_DIM
    lse_lane = lax.broadcasted_iota(jnp.int32, (blk, LANES), 1)
    units = [(rr, jb) for rr in range(nr) for jb in range(nq)]

    scores = []
    for rr, jb in units:
        lanes = slice(rr * W, (rr + 1) * W)
        q = q_ref[jb * blk:(jb + 1) * blk, lanes] * (ATTN_HEAD_DIM ** -0.5)
        qs = jnp.concatenate([jnp.where(lane_head == h, q, jnp.zeros_like(q)) for h in range(H)], axis=0)
        kwin = (jnp.concatenate([kp_ref[:, lanes], kc_ref[0:blk, lanes]], axis=0) if jb == 0
                else kc_ref[(jb - 1) * blk:(jb + 1) * blk, lanes])
        scores.append(lax.dot_general(qs, kwin, (((1,), (1,)), ((), ())), preferred_element_type=F32))
    probs, maxes = [], []
    for (rr, jb), sc in zip(units, scores):
        valid = band_first if jb == 0 else band
        for h in range(H):
            s = jnp.where(valid, sc[h * blk:(h + 1) * blk], -jnp.inf)
            m = jnp.max(s, axis=-1, keepdims=True)
            probs.append(jnp.exp((s - m).astype(BF16)))
            maxes.append(m)
    ones = jnp.ones((2 * blk, LANES), BF16)
    for u, (rr, jb) in enumerate(units):
        lanes = slice(rr * W, (rr + 1) * W)
        vwin = (jnp.concatenate([vp_ref[:, lanes], vc_ref[0:blk, lanes]], axis=0) if jb == 0
                else vc_ref[(jb - 1) * blk:(jb + 1) * blk, lanes])
        pstack = jnp.concatenate(probs[u * H:(u + 1) * H], axis=0)
        pv = jnp.dot(pstack, vwin, preferred_element_type=F32)
        den = jnp.dot(pstack, ones, preferred_element_type=F32)
        o_acc = jnp.zeros((blk, W), F32)
        lse_acc = jnp.zeros((blk, LANES), F32)
        for h in range(H):
            den_h = den[h * blk:(h + 1) * blk]
            rden = 1.0 / den_h
            o_h = pv[h * blk:(h + 1) * blk] * jnp.concatenate([rden] * (W // LANES), axis=1)
            o_acc = jnp.where(lane_head == h, o_h, o_acc)
            lse_acc = jnp.where(lse_lane == h, maxes[u * H + h] + jnp.log(den_h), lse_acc)
        o_ref[jb * blk:(jb + 1) * blk, lanes] = o_acc.astype(o_ref.dtype)
        lse_ref[jb * blk:(jb + 1) * blk, rr * LANES:(rr + 1) * LANES] = lse_acc


def _dilated_attention(q, k, v, batch, seq, window, dilation):
    d = dilation
    L = seq // d
    W = ATTN_GROUP_WIDTH
    nq = min(ATTN_STEP_BLOCKS, L // ATTN_BLOCK)
    nr = min(ATTN_STEP_BLOCKS // nq, d)
    qrows = nq * ATTN_BLOCK
    assert L % qrows == 0 and d % nr == 0
    view = lambda t: t.reshape(batch, L, t.shape[-1])
    cur = lambda b, r, i: (b, i, r)
    prev = lambda b, r, i: (b, jnp.maximum(i * nq - 1, 0), r)
    cur_spec = pl.BlockSpec((None, qrows, nr * W), cur)
    prev_spec = pl.BlockSpec((None, ATTN_BLOCK, nr * W), prev)
    o, lse = pl.pallas_call(
        functools.partial(_attn_body, span=window // d),
        grid=(batch, d // nr, L // qrows),
        in_specs=[cur_spec, prev_spec, cur_spec, prev_spec, cur_spec],
        out_specs=[cur_spec, pl.BlockSpec((None, qrows, nr * LANES), cur)],
        out_shape=[jax.ShapeDtypeStruct((batch, L, d * W), BF16),
                   jax.ShapeDtypeStruct((batch, L, d * LANES), F32)],
        compiler_params=_params("parallel", "parallel", "parallel"),
        name=f"dilated_attention_d{d}",
    )(view(q), view(k), view(k), view(v), view(v))
    return o.reshape(batch * L, d * W), lse.reshape(batch * L, d * LANES)


def _mm(a, b):
    return jnp.dot(a, b, preferred_element_type=F32)


def _mm_nt(a, b):
    return lax.dot_general(a, b, (((1,), (1,)), ((), ())), preferred_element_type=F32)


def _mm_tn(a, b):
    return lax.dot_general(a, b, (((0,), (0,)), ((), ())), preferred_element_type=F32)


def _deltanet_body(x_ref, sm_ref, z_ref, avec_ref, dtb_ref, onw_ref, o_ref, state_ref):
    C = DN_CHUNK
    Dh = DN_HEAD_DIM
    rows = x_ref.shape[0]
    blk = pl.program_id(1)

    @pl.when(blk == 0)
    def _():
        state_ref[...] = jnp.zeros_like(state_ref)

    sm = sm_ref[...]
    beta_all = _sigmoid(sm)
    sp_in = sm + dtb_ref[...]
    softplus = jnp.maximum(sp_in, 0.0) + jnp.log(1.0 + jnp.exp(-jnp.abs(sp_in)))
    g_all = avec_ref[...] * softplus
    row_in_chunk = lax.broadcasted_iota(jnp.int32, (rows, LANES), 0) % C
    gc_all = g_all
    s = 1
    while s < C:
        gc_all = gc_all + jnp.where(row_in_chunk >= s, pltpu.roll(gc_all, s, axis=0), 0.0)
        s *= 2
    sel = jnp.where(lax.broadcasted_iota(jnp.int32, (SUBLANES_F32, LANES), 1)
                    == lax.broadcasted_iota(jnp.int32, (SUBLANES_F32, LANES), 0) + DN_HEADS, 1.0, 0.0).astype(F32)

    ri = lax.broadcasted_iota(jnp.int32, (C, C), 0)
    ci = lax.broadcasted_iota(jnp.int32, (C, C), 1)
    incl = ri >= ci
    strict = ri > ci
    same_block = (ri // DN_SOLVE_BLOCK) == (ci // DN_SOLVE_BLOCK)
    nc = rows // C

    items = []
    for c in range(nc):
        rs = slice(c * C, (c + 1) * C)
        gc_c = gc_all[rs]
        gc_rows = lax.dot_general(sel, gc_c, (((1,), (1,)), ((), ())), preferred_element_type=F32,
                                  precision=HIGHEST)
        for h in range(DN_HEADS):
            q_b = x_ref[rs, h * Dh:(h + 1) * Dh]
            k_b = x_ref[rs, (DN_HEADS + h) * Dh:(DN_HEADS + h + 1) * Dh]
            k = k_b.astype(F32)
            v = x_ref[rs, (2 * DN_HEADS + h) * Dh:(2 * DN_HEADS + h + 1) * Dh].astype(F32)
            beta = beta_all[rs, h:h + 1]
            gcol = gc_c[:, DN_HEADS + h:DN_HEADS + h + 1]
            grow = gc_rows[h:h + 1, :]
            decay = jnp.exp(jnp.where(incl, gcol - grow, -jnp.inf))
            kb = k * beta
            eg = jnp.exp(gcol)
            g_last = gcol[C - 1:C, :]
            qk_kk = _mm_nt(jnp.concatenate([q_b, kb.astype(BF16)], axis=0), k_b)
            m = jnp.where(strict, qk_kk[C:] * decay, 0.0)
            md = jnp.where(same_block, m, 0.0)
            items.append(dict(
                rs=rs, h=h, qk=(qk_kk[:C] * decay).astype(BF16), md=md, mo=m - md,
                rhs=jnp.concatenate([v * beta, kb * eg], axis=1),
                qe=q_b.astype(F32) * eg, k_dec=(k * jnp.exp(g_last - gcol)).astype(BF16),
                g_last=jnp.exp(g_last)))

    for it in items:
        md_b = it["md"].astype(BF16)
        it["p"] = -it["md"]
        it["mk"] = _mm(md_b, md_b)
    for step in range(3):
        for it in items:
            mk_b = it["mk"].astype(BF16)
            p_b = it["p"].astype(BF16)
            if step < 2:
                both = _mm(jnp.concatenate([mk_b, p_b], axis=0), mk_b)
                it["p"] = it["p"] + it["mk"] + both[C:]
                it["mk"] = both[:C]
            else:
                it["p"] = it["p"] + it["mk"] + _mm(p_b, mk_b)
    for it in items:
        it["p_b"] = it["p"].astype(BF16)
        it["n"] = it["mo"] + _mm(it["p_b"], it["mo"].astype(BF16))
    for it in items:
        it["n_b"] = it["n"].astype(BF16)
        it["n2"] = _mm(it["n_b"], it["n_b"])
    for it in items:
        it["q"] = it["n2"] - it["n"] - _mm(it["n2"].astype(BF16), it["n_b"])
    for it in items:
        it["tinv"] = it["q"] + it["p"] + _mm(it["q"].astype(BF16), it["p_b"])
    for it in items:
        uw = it["rhs"] + _mm(it["tinv"].astype(BF16), it["rhs"].astype(BF16))
        it["uw_b"] = uw.astype(BF16)
    for it in items:
        ktuw = _mm_tn(it["k_dec"], it["uw_b"])
        it["ktu"] = ktuw[:, :Dh]
        it["ktw_b"] = ktuw[:, Dh:].astype(BF16)
    for it in items:
        quw = _mm(it["qk"], it["uw_b"])
        it["o0"] = quw[:, :Dh]
        it["q_eff_b"] = (it["qe"] - quw[:, Dh:]).astype(BF16)

    states = [state_ref[h] for h in range(DN_HEADS)]
    for c in range(nc):
        chunk_items = items[c * DN_HEADS:(c + 1) * DN_HEADS]
        prods = [_mm(jnp.concatenate([it["ktw_b"], it["q_eff_b"]], axis=0), states[it["h"]].astype(BF16))
                 for it in chunk_items]
        for it, prod in zip(chunk_items, prods):
            h, rs = it["h"], it["rs"]
            o = prod[Dh:] + it["o0"]
            states[h] = states[h] * it["g_last"] + it["ktu"] - prod[:Dh]
            zg = z_ref[rs, h * Dh:(h + 1) * Dh].astype(F32)
            o_ref[rs, h * Dh:(h + 1) * Dh] = (_rms_norm(o, onw_ref[...]) * _silu(zg)).astype(o_ref.dtype)
    for h in range(DN_HEADS):
        state_ref[h] = states[h]


def _deltanet(dn_qkv, small, z, a_log, dt_bias, onorm_w, batch, seq):
    T = dn_qkv.shape[0]
    rows = DN_BLOCK_CHUNKS * DN_CHUNK
    nblocks = seq // rows
    lane = jnp.arange(LANES)
    in_decay_lanes = (lane >= DN_HEADS) & (lane < 2 * DN_HEADS)
    idx = jnp.clip(lane - DN_HEADS, 0, DN_HEADS - 1)
    avec = jnp.where(in_decay_lanes, -jnp.exp(a_log.astype(F32))[idx], 0.0).reshape(1, LANES)
    dtb = jnp.where(in_decay_lanes, dt_bias.astype(F32)[idx], 0.0).reshape(1, LANES)
    row = lambda b, c: (b * nblocks + c, 0)
    return pl.pallas_call(
        _deltanet_body,
        grid=(batch, nblocks),
        in_specs=[pl.BlockSpec((rows, 3 * DN_WIDTH), row),
                  pl.BlockSpec((rows, LANES), row),
                  pl.BlockSpec((rows, DN_WIDTH), row),
                  _resident((1, LANES)), _resident((1, LANES)), _resident((1, DN_HEAD_DIM))],
        out_specs=pl.BlockSpec((rows, DN_WIDTH), row),
        out_shape=jax.ShapeDtypeStruct((T, DN_WIDTH), BF16),
        scratch_shapes=[pltpu.VMEM((DN_HEADS, DN_HEAD_DIM, DN_HEAD_DIM), F32)],
        compiler_params=_params("parallel", "arbitrary"),
        name="gated_deltanet",
    )(dn_qkv, small, z, avec, dtb, onorm_w.astype(F32).reshape(1, DN_HEAD_DIM))


def _mixer_output(o_refs, lse_refs, relay_refs, dilations, ob_ref, gates_ref, x_ref, expand_ref, wpa_ref, wpb_ref,
                  wo_ref):
    D = x_ref.shape[-1]
    relay_refs = list(relay_refs)
    o_nat, lses = [], []
    for o_ref, lse_ref, d in zip(o_refs, lse_refs, dilations):
        if d == 1:
            o_nat.append(o_ref[...].astype(F32))
            lses.append(lse_ref[...])
        else:
            o_nat.append(_from_dilated(o_ref, relay_refs.pop(0), d))
            lses.append(_from_dilated(lse_ref, relay_refs.pop(0), d))
    mx = functools.reduce(jnp.maximum, lses)
    es = [jnp.exp(l - mx) for l in lses]
    inv = 1.0 / functools.reduce(lambda a, b: a + b, es)
    o_a = None
    for e, o_g in zip(es, o_nat):
        alpha = e * inv
        hi = alpha.astype(BF16)
        lo = (alpha - hi.astype(F32)).astype(BF16)
        alpha_wide = jnp.dot(jnp.concatenate([hi, lo], axis=1), expand_ref[...], preferred_element_type=F32)
        term = alpha_wide * o_g
        o_a = term if o_a is None else o_a + term
    pa = jnp.dot(o_a.astype(BF16), wpa_ref[...], preferred_element_type=F32)
    pb = jnp.dot(ob_ref[...], wpb_ref[...], preferred_element_type=F32)
    ga = gates_ref[:, :D].astype(F32)
    gb = gates_ref[:, D:].astype(F32)
    y = _sigmoid(ga) * pa + _sigmoid(gb) * pb
    return x_ref[...] + jnp.dot(y.astype(BF16), wo_ref[...], preferred_element_type=F32)


def _mixer_ffn_body(*refs, dilations, tiles_per_seq, col_tile, final_norm):
    ng = len(dilations)
    o_refs = refs[:ng]
    lse_refs = refs[ng:2 * ng]
    (ob_ref, gates_ref, x_ref, expand_ref, wpa_ref, wpb_ref, wo_ref,
     nw_ref, wup_ref, cw_ref, cb_ref, wdown_ref, fw_ref, out_ref, act_ref, hcarry_ref) = refs[2 * ng:2 * ng + 16]
    relay_refs = refs[2 * ng + 16:]
    dff = wdown_ref.shape[0]
    halo_rows = hcarry_ref.shape[0]
    rows = x_ref.shape[0]
    i = pl.program_id(0)

    @pl.when(i % tiles_per_seq == 0)
    def _():
        hcarry_ref[...] = jnp.zeros_like(hcarry_ref)

    h = _mixer_output(o_refs, lse_refs, relay_refs, dilations, ob_ref, gates_ref, x_ref, expand_ref, wpa_ref,
                      wpb_ref, wo_ref)
    left = hcarry_ref[...]
    hcarry_ref[...] = h[rows - halo_rows:]
    hn = _rms_norm(jnp.concatenate([left, h], axis=0), nw_ref[...]).astype(BF16)

    def conv(c0):
        u = jnp.dot(hn, wup_ref[:, c0:c0 + col_tile], preferred_element_type=F32)
        acc = u * cw_ref[FFN_CONV - 1:FFN_CONV, c0:c0 + col_tile]
        for j in range(1, FFN_CONV):
            acc = acc + pltpu.roll(u, j, axis=0) * cw_ref[FFN_CONV - 1 - j:FFN_CONV - j, c0:c0 + col_tile]
        return acc[halo_rows:] + cb_ref[:, c0:c0 + col_tile]

    for c0 in range(0, dff, col_tile):
        act_ref[:, c0:c0 + col_tile] = (_silu(conv(c0)) * conv(dff + c0)).astype(BF16)
    y = h + jnp.dot(act_ref[...], wdown_ref[...], preferred_element_type=F32)
    if final_norm:
        y = _rms_norm(y, fw_ref[...])
    out_ref[...] = y


def _mixer_ffn(o_groups, lse_groups, dilations, o_b, gates, x2d, layer, w_pa, w_pb, w_o, seq, norm_w, w_up, conv_w,
               conv_b, w_down, final_w, final_norm):
    T, D = x2d.shape
    W = ATTN_GROUP_WIDTH
    dff = w_down.shape[1]
    col_tile = 2 * LANES
    lane = jnp.arange(2 * LANES)[:, None] % LANES
    expand = (lane == (jnp.arange(W)[None, :] // ATTN_HEAD_DIM)).astype(BF16)
    row = lambda i: (i, 0)
    tile = lambda n, d=1: pl.BlockSpec((ROW_TILE // d, d * n), row)
    relay = []
    for d in dilations:
        if d > 1:
            relay += [pltpu.VMEM((W // LANES, ROW_TILE, LANES), F32), pltpu.VMEM((1, ROW_TILE, LANES), F32)]
    body = functools.partial(_mixer_ffn_body, dilations=tuple(dilations), tiles_per_seq=seq // ROW_TILE,
                             col_tile=col_tile, final_norm=final_norm)
    stacks = (w_pa, w_pb, w_o, norm_w, w_up, conv_w, conv_b, w_down)
    return pl.pallas_call(
        body,
        grid=(T // ROW_TILE,),
        in_specs=[tile(W, d) for d in dilations] + [tile(LANES, d) for d in dilations]
                 + [tile(DN_WIDTH), tile(2 * D), tile(D), _resident(expand.shape)]
                 + [_layer_resident(w, layer) for w in stacks] + [_resident((1, D))],
        out_specs=tile(D),
        out_shape=jax.ShapeDtypeStruct((T, D), F32),
        scratch_shapes=[pltpu.VMEM((ROW_TILE, dff), BF16), pltpu.VMEM((SUBLANES_F32, D), F32)] + relay,
        compiler_params=_params("arbitrary"),
        name="mixer_out_conv_glu_ffn",
    )(*o_groups, *lse_groups, o_b, gates, x2d, expand, *stacks, final_w.reshape(1, D))


def _split_in_weights(w_in):
    ng = len(ATTN_GROUPS)
    W = ATTN_GROUP_WIDTH
    aw = ng * W
    n_main = 3 * aw + 3 * DN_WIDTH
    n_small = 2 * DN_HEADS
    D = w_in.shape[1]
    main = w_in[:, :, :n_main].astype(BF16)
    small = jnp.pad(w_in[:, :, n_main:n_main + n_small], ((0, 0), (0, 0), (0, LANES - n_small)))
    tail = jnp.concatenate([w_in[:, :, n_main + n_small:], small], axis=2).astype(BF16)
    columns = [(0, part * aw + g * W) for g in range(ng) for part in range(3)]
    columns += [(0, 3 * aw), (1, 0), (1, DN_WIDTH), (1, DN_WIDTH + 2 * D)]
    widths = [W] * (3 * ng) + [3 * DN_WIDTH, DN_WIDTH, 2 * D, LANES]
    dtypes = [BF16] * (3 * ng + 3) + [F32]
    return main, tail, columns, widths, dtypes


def kernel(x, norm1_w, w_in, dn_conv_w, dn_a_log, dn_dt_bias, dn_onorm_w, w_pa, w_pb, w_o, norm2_w, w_up,
           ffn_conv_w, ffn_conv_b, w_down, final_norm_w):
    B, S, D = x.shape
    depth = w_in.shape[0]
    assert S % ROW_TILE == 0 and S % (DN_BLOCK_CHUNKS * DN_CHUNK) == 0
    assert all(ROW_TILE % (SUBLANES_BF16 * d) == 0 for _, d in ATTN_GROUPS)
    xf = x.astype(F32).reshape(B * S, D)
    w_main, w_tail, columns, widths, dtypes = _split_in_weights(w_in)
    row_stack = lambda p: p.astype(F32).reshape(depth, 1, -1)
    norm1_s, norm2_s, conv_b_s = row_stack(norm1_w), row_stack(norm2_w), row_stack(ffn_conv_b)
    dn_conv_s, ffn_conv_s = dn_conv_w.astype(F32), ffn_conv_w.astype(F32)
    w_pa_s, w_pb_s, w_o_s, w_up_s, w_down_s = (w.astype(BF16) for w in (w_pa, w_pb, w_o, w_up, w_down))
    ng = len(ATTN_GROUPS)
    group_dilations = [d for _, d in ATTN_GROUPS]
    out_dilations = [d for d in group_dilations for _ in range(3)] + [1] * (len(widths) - 3 * ng)
    for l in range(depth):
        outs = _in_projection(xf, l, norm1_s, w_main, w_tail, dn_conv_s, widths, dtypes, out_dilations, columns,
                              dn_index=3 * ng, seq=S)
        o_groups, lse_groups = [], []
        for g, (window, dilation) in enumerate(ATTN_GROUPS):
            q, k, v = outs[3 * g:3 * g + 3]
            o_g, lse_g = _dilated_attention(q, k, v, B, S, window, dilation)
            o_groups.append(o_g)
            lse_groups.append(lse_g)
        dn_qkv, z, gates, small = outs[3 * ng:]
        o_b = _deltanet(dn_qkv, small, z, dn_a_log[l], dn_dt_bias[l], dn_onorm_w[l], B, S)
        xf = _mixer_ffn(o_groups, lse_groups, group_dilations, o_b, gates, xf, l, w_pa_s, w_pb_s, w_o_s, S, norm2_s,
                        w_up_s, ffn_conv_s, conv_b_s, w_down_s, final_norm_w.astype(F32),
                        final_norm=(l == depth - 1))
    return xf.reshape(B, S, D).astype(x.dtype)
```

```python
import functools

import jax
import jax.numpy as jnp
from jax import lax
from jax.experimental import pallas as pl
from jax.experimental.pallas import tpu as pltpu

F32 = jnp.float32
BF16 = jnp.bfloat16
HIGHEST = lax.Precision.HIGHEST

NORM_EPS = 1e-6

ATTN_HEAD_DIM = 64
ATTN_HEADS_PER_GROUP = 4
ATTN_GROUP_WIDTH = ATTN_HEADS_PER_GROUP * ATTN_HEAD_DIM
ATTN_GROUPS = ((128, 1), (512, 4), (2048, 16))
ATTN_BLOCK = 128
ATTN_STEP_BLOCKS = 4
DN_HEADS = 4
DN_HEAD_DIM = 128
DN_WIDTH = DN_HEADS * DN_HEAD_DIM
DN_CONV = 4
DN_CHUNK = 64
DN_BLOCK_CHUNKS = 8
DN_SOLVE_BLOCK = 16
FFN_CONV = 3

LANES = 128
SUBLANES_F32 = 8
SUBLANES_BF16 = 16
VMEM_LIMIT_BYTES = 56 * 1024 * 1024

ROW_TILE = 512


def _resident(shape):
    nd = len(shape)
    return pl.BlockSpec(shape, lambda *_: (0,) * nd, pipeline_mode=pl.Buffered(1))


def _layer_resident(stacked, layer):
    nd = stacked.ndim - 1
    return pl.BlockSpec((None,) + stacked.shape[1:], lambda *_: (layer,) + (0,) * nd, pipeline_mode=pl.Buffered(1))


def _params(*semantics):
    return pltpu.CompilerParams(dimension_semantics=semantics, vmem_limit_bytes=VMEM_LIMIT_BYTES)


def _rms_norm(x, w):
    return x * lax.rsqrt(jnp.mean(x * x, axis=-1, keepdims=True) + NORM_EPS) * w


def _silu(x):
    return x * (1.0 / (1.0 + jnp.exp(-x)))


def _sigmoid(x):
    return 1.0 / (1.0 + jnp.exp(-x))


def _to_dilated(y, relay_ref, o_ref, d):
    rows = y.shape[0]
    for half in range(ATTN_GROUP_WIDTH // LANES):
        relay_ref[half] = y[:, half * LANES:(half + 1) * LANES]
    for r in range(d):
        for half in range(ATTN_GROUP_WIDTH // LANES):
            c0 = r * ATTN_GROUP_WIDTH + half * LANES
            o_ref[:, c0:c0 + LANES] = relay_ref[half, pl.ds(r, rows // d, stride=d), :].astype(o_ref.dtype)


def _from_dilated(x_ref, relay_ref, d):
    rows = relay_ref.shape[1]
    nslab = relay_ref.shape[0]
    width = nslab * LANES
    for r in range(d):
        for slab in range(nslab):
            c0 = r * width + slab * LANES
            relay_ref[slab, pl.ds(r, rows // d, stride=d), :] = x_ref[:, c0:c0 + LANES].astype(F32)
    return jnp.concatenate([relay_ref[slab] for slab in range(nslab)], axis=1)


def _inproj_body(x_ref, nw_ref, wmain_ref, wtail_ref, cw_ref, *refs, dilations, columns, dn_index, tiles_per_seq):
    n_out = len(dilations)
    out_refs = refs[:n_out]
    ydn_ref = refs[n_out]
    relay_refs = refs[n_out + 1:]
    w_refs = (wmain_ref, wtail_ref)
    i = pl.program_id(0)
    xn = _rms_norm(x_ref[...], nw_ref[...]).astype(BF16)

    halo = ydn_ref.shape[0] - x_ref.shape[0]
    rows = x_ref.shape[0]
    dn_ref = out_refs[dn_index]

    @pl.when(i % tiles_per_seq == 0)
    def _():
        ydn_ref[0:halo, :] = jnp.zeros((halo, ydn_ref.shape[1]), F32)

    @pl.when(i % tiles_per_seq > 0)
    def _():
        ydn_ref[0:halo, :] = ydn_ref[rows:rows + halo, :]

    step = 2 * DN_HEAD_DIM
    dn_w, dn_c0 = columns[dn_index]
    dn_pieces = list(range(0, 3 * DN_WIDTH, step))

    def dn_piece(cb):
        ydn_ref[halo:, cb:cb + step] = jnp.dot(xn, w_refs[dn_w][:, dn_c0 + cb:dn_c0 + cb + step],
                                               preferred_element_type=F32)
        for col in range(cb, cb + step, DN_HEAD_DIM):
            _dn_conv_block(ydn_ref, cw_ref, dn_ref, col, halo)

    others = [(j, p0) for j, (o_ref, d) in enumerate(zip(out_refs, dilations)) if j != dn_index
              for p0 in range(0, o_ref.shape[-1] // d, step)]
    every = max(1, len(others) // (len(dn_pieces) + 1))
    n_relay = 0
    for t, (j, p0) in enumerate(others):
        o_ref, d = out_refs[j], dilations[j]
        n = o_ref.shape[-1] // d
        pn = min(step, n - p0)
        wi, c0 = columns[j]
        y = jnp.dot(xn, w_refs[wi][:, c0 + p0:c0 + p0 + pn], preferred_element_type=F32)
        if d == 1:
            o_ref[:, p0:p0 + pn] = y.astype(o_ref.dtype)
        else:
            _to_dilated(y, relay_refs[n_relay], o_ref, d)
            n_relay += 1
        if dn_pieces and (t + 1) % every == 0:
            dn_piece(dn_pieces.pop(0))
    while dn_pieces:
        dn_piece(dn_pieces.pop(0))


def _dn_conv_block(ydn_ref, cw_ref, o_ref, col, halo):
    Dh = DN_HEAD_DIM
    rows = o_ref.shape[0]
    lanes = slice(col, col + Dh)
    acc = ydn_ref[halo:, lanes] * cw_ref[DN_CONV - 1:DN_CONV, lanes]
    for j in range(1, DN_CONV):
        acc = acc + ydn_ref[halo - j:halo - j + rows, lanes] * cw_ref[DN_CONV - 1 - j:DN_CONV - j, lanes]
    a = _silu(acc)
    if col < 2 * DN_WIDTH:
        a = a * lax.rsqrt(jnp.sum(a * a, axis=-1, keepdims=True) + NORM_EPS)
    if col < DN_WIDTH:
        a = a * (Dh ** -0.5)
    o_ref[:, lanes] = a.astype(o_ref.dtype)


def _in_projection(x2d, layer, norm_w, w_main, w_tail, dn_conv_w, out_widths, out_dtypes, dilations, columns,
                   dn_index, seq):
    T, D = x2d.shape
    grid = (T // ROW_TILE,)
    row = lambda i: (i, 0)
    n_relay = sum(1 for d in dilations if d > 1)
    body = functools.partial(_inproj_body, dilations=tuple(dilations), columns=tuple(columns), dn_index=dn_index,
                             tiles_per_seq=seq // ROW_TILE)
    return pl.pallas_call(
        body,
        grid=grid,
        in_specs=[pl.BlockSpec((ROW_TILE, D), row), _layer_resident(norm_w, layer), _layer_resident(w_main, layer),
                  _layer_resident(w_tail, layer), _layer_resident(dn_conv_w, layer)],
        out_specs=[pl.BlockSpec((ROW_TILE // d, d * n), row) for n, d in zip(out_widths, dilations)],
        out_shape=[jax.ShapeDtypeStruct((T // d, d * n), dt)
                   for n, dt, d in zip(out_widths, out_dtypes, dilations)],
        scratch_shapes=[pltpu.VMEM((SUBLANES_F32 + ROW_TILE, 3 * DN_WIDTH), F32)]
                       + [pltpu.VMEM((ATTN_GROUP_WIDTH // LANES, ROW_TILE, LANES), F32)] * n_relay,
        compiler_params=_params("arbitrary"),
        name="in_projection",
    )(x2d, norm_w, w_main, w_tail, dn_conv_w)


def _attn_body(*refs, spans, blocks_per_seq):
    ng = len(spans)
    blk = ATTN_BLOCK
    W = ATTN_GROUP_WIDTH
    H = ATTN_HEADS_PER_GROUP
    step = pl.program_id(1)
    row = lax.broadcasted_iota(jnp.int32, (blk, 2 * blk), 0)
    col = lax.broadcasted_iota(jnp.int32, (blk, 2 * blk), 1)
    rel = row + blk - col
    lane_head = lax.broadcasted_iota(jnp.int32, (blk, W), 1) // ATTN_HEAD_DIM
    lse_lane = lax.broadcasted_iota(jnp.int32, (blk, LANES), 1)

    units = []
    for g in range(ng):
        q_ref, kp_ref, kc_ref, vp_ref, vc_ref = refs[5 * g:5 * g + 5]
        o_ref, lse_ref = refs[5 * ng + 2 * g:5 * ng + 2 * g + 2]
        band = (rel >= 0) & (rel <= spans[g])
        first_block = step % blocks_per_seq[g] == 0
        band_first = band & (col >= jnp.where(first_block, blk, 0))
        for rr in range(q_ref.shape[1] // W):
            for jb in range(q_ref.shape[0] // blk):
                units.append(dict(q=q_ref, kp=kp_ref, kc=kc_ref, vp=vp_ref, vc=vc_ref, o=o_ref, lse=lse_ref,
                                  rr=rr, jb=jb, lanes=slice(rr * W, (rr + 1) * W),
                                  valid=band_first if jb == 0 else band))

    def window(prev_ref, cur_ref, u):
        jb, lanes = u["jb"], u["lanes"]
        if jb == 0:
            return jnp.concatenate([prev_ref[:, lanes], cur_ref[0:blk, lanes]], axis=0)
        return cur_ref[(jb - 1) * blk:(jb + 1) * blk, lanes]

    for u in units:
        jb = u["jb"]
        q = u["q"][jb * blk:(jb + 1) * blk, u["lanes"]] * (ATTN_HEAD_DIM ** -0.5)
        qs = jnp.concatenate([jnp.where(lane_head == h, q, jnp.zeros_like(q)) for h in range(H)], axis=0)
        u["scores"] = lax.dot_general(qs, window(u["kp"], u["kc"], u), (((1,), (1,)), ((), ())),
                                      preferred_element_type=F32)
    for u in units:
        probs, maxes = [], []
        for h in range(H):
            s = jnp.where(u["valid"], u["scores"][h * blk:(h + 1) * blk], -jnp.inf)
            m = jnp.max(s, axis=-1, keepdims=True)
            probs.append(jnp.exp((s - m).astype(BF16)))
            maxes.append(m)
        u["probs"] = jnp.concatenate(probs, axis=0)
        u["maxes"] = maxes
    ones = jnp.ones((2 * blk, LANES), BF16)
    for u in units:
        jb = u["jb"]
        pv = jnp.dot(u["probs"], window(u["vp"], u["vc"], u), preferred_element_type=F32)
        den = jnp.dot(u["probs"], ones, preferred_element_type=F32)
        o_acc = jnp.zeros((blk, W), F32)
        lse_acc = jnp.zeros((blk, LANES), F32)
        for h in range(H):
            den_h = den[h * blk:(h + 1) * blk]
            rden = 1.0 / den_h
            o_h = pv[h * blk:(h + 1) * blk] * jnp.concatenate([rden] * (W // LANES), axis=1)
            o_acc = jnp.where(lane_head == h, o_h, o_acc)
            lse_acc = jnp.where(lse_lane == h, u["maxes"][h] + jnp.log(den_h), lse_acc)
        u["o"][jb * blk:(jb + 1) * blk, u["lanes"]] = o_acc.astype(u["o"].dtype)
        u["lse"][jb * blk:(jb + 1) * blk, u["rr"] * LANES:(u["rr"] + 1) * LANES] = lse_acc


def _dilated_attention(qkv_groups, batch, seq):
    W = ATTN_GROUP_WIDTH
    operands, in_specs, out_specs, out_shape, spans, blocks_per_seq, steps = [], [], [], [], [], [], None
    for (window, d), (q, k, v) in zip(ATTN_GROUPS, qkv_groups):
        L = seq // d
        nq = min(ATTN_STEP_BLOCKS, L // ATTN_BLOCK)
        nr = min(ATTN_STEP_BLOCKS // nq, d)
        qrows = nq * ATTN_BLOCK
        n_i = L // qrows
        assert L % qrows == 0 and d % nr == 0
        assert steps in (None, (d // nr) * n_i)
        steps = (d // nr) * n_i
        view = lambda t, L=L: t.reshape(batch, L, t.shape[-1])
        cur = lambda b, s, n_i=n_i: (b, s % n_i, s // n_i)
        prev = lambda b, s, n_i=n_i, nq=nq: (b, jnp.maximum((s % n_i) * nq - 1, 0), s // n_i)
        cur_spec = pl.BlockSpec((None, qrows, nr * W), cur)
        prev_spec = pl.BlockSpec((None, ATTN_BLOCK, nr * W), prev)
        operands += [view(q), view(k), view(k), view(v), view(v)]
        in_specs += [cur_spec, prev_spec, cur_spec, prev_spec, cur_spec]
        out_specs += [cur_spec, pl.BlockSpec((None, qrows, nr * LANES), cur)]
        out_shape += [jax.ShapeDtypeStruct((batch, L, d * W), BF16), jax.ShapeDtypeStruct((batch, L, d * LANES), F32)]
        spans.append(window // d)
        blocks_per_seq.append(n_i)
    outs = pl.pallas_call(
        functools.partial(_attn_body, spans=tuple(spans), blocks_per_seq=tuple(blocks_per_seq)),
        grid=(batch, steps),
        in_specs=in_specs,
        out_specs=out_specs,
        out_shape=out_shape,
        compiler_params=_params("parallel", "parallel"),
        name="dilated_attention",
    )(*operands)
    o_groups = [o.reshape(-1, o.shape[-1]) for o in outs[0::2]]
    lse_groups = [l.reshape(-1, l.shape[-1]) for l in outs[1::2]]
    return o_groups, lse_groups


def _mm(a, b):
    return jnp.dot(a, b, preferred_element_type=F32)


def _mm_nt(a, b):
    return lax.dot_general(a, b, (((1,), (1,)), ((), ())), preferred_element_type=F32)


def _mm_tn(a, b):
    return lax.dot_general(a, b, (((0,), (0,)), ((), ())), preferred_element_type=F32)


def _deltanet_body(x_ref, sm_ref, z_ref, avec_ref, dtb_ref, onw_ref, o_ref, state_ref):
    C = DN_CHUNK
    Dh = DN_HEAD_DIM
    rows = x_ref.shape[0]
    blk = pl.program_id(1)

    @pl.when(blk == 0)
    def _():
        state_ref[...] = jnp.zeros_like(state_ref)

    sm = sm_ref[...]
    beta_all = _sigmoid(sm)
    sp_in = sm + dtb_ref[...]
    softplus = jnp.maximum(sp_in, 0.0) + jnp.log(1.0 + jnp.exp(-jnp.abs(sp_in)))
    g_all = avec_ref[...] * softplus
    row_in_chunk = lax.broadcasted_iota(jnp.int32, (rows, LANES), 0) % C
    gc_all = g_all
    s = 1
    while s < C:
        gc_all = gc_all + jnp.where(row_in_chunk >= s, pltpu.roll(gc_all, s, axis=0), 0.0)
        s *= 2
    sel = jnp.where(lax.broadcasted_iota(jnp.int32, (SUBLANES_F32, LANES), 1)
                    == lax.broadcasted_iota(jnp.int32, (SUBLANES_F32, LANES), 0) + DN_HEADS, 1.0, 0.0).astype(F32)

    ri = lax.broadcasted_iota(jnp.int32, (C, C), 0)
    ci = lax.broadcasted_iota(jnp.int32, (C, C), 1)
    incl = ri >= ci
    strict = ri > ci
    same_block = (ri // DN_SOLVE_BLOCK) == (ci // DN_SOLVE_BLOCK)
    nc = rows // C

    items = []
    for c in range(nc):
        rs = slice(c * C, (c + 1) * C)
        gc_c = gc_all[rs]
        gc_rows = lax.dot_general(sel, gc_c, (((1,), (1,)), ((), ())), preferred_element_type=F32,
                                  precision=HIGHEST)
        for h in range(DN_HEADS):
            q_b = x_ref[rs, h * Dh:(h + 1) * Dh]
            k_b = x_ref[rs, (DN_HEADS + h) * Dh:(DN_HEADS + h + 1) * Dh]
            k = k_b.astype(F32)
            v = x_ref[rs, (2 * DN_HEADS + h) * Dh:(2 * DN_HEADS + h + 1) * Dh].astype(F32)
            beta = beta_all[rs, h:h + 1]
            gcol = gc_c[:, DN_HEADS + h:DN_HEADS + h + 1]
            grow = gc_rows[h:h + 1, :]
            decay = jnp.exp(jnp.where(incl, gcol - grow, -jnp.inf))
            kb = k * beta
            eg = jnp.exp(gcol)
            g_last = gcol[C - 1:C, :]
            qk_kk = _mm_nt(jnp.concatenate([q_b, kb.astype(BF16)], axis=0), k_b)
            m = jnp.where(strict, qk_kk[C:] * decay, 0.0)
            md = jnp.where(same_block, m, 0.0)
            items.append(dict(
                rs=rs, h=h, qk=(qk_kk[:C] * decay).astype(BF16), md=md, mo=m - md,
                rhs=jnp.concatenate([v * beta, kb * eg], axis=1),
                qe=q_b.astype(F32) * eg, k_dec=(k * jnp.exp(g_last - gcol)).astype(BF16),
                g_last=jnp.exp(g_last)))

    for it in items:
        md_b = it["md"].astype(BF16)
        it["p"] = -it["md"]
        it["mk"] = _mm(md_b, md_b)
    for step in range(3):
        for it in items:
            mk_b = it["mk"].astype(BF16)
            p_b = it["p"].astype(BF16)
            if step < 2:
                both = _mm(jnp.concatenate([mk_b, p_b], axis=0), mk_b)
                it["p"] = it["p"] + it["mk"] + both[C:]
                it["mk"] = both[:C]
            else:
                it["p"] = it["p"] + it["mk"] + _mm(p_b, mk_b)
    for it in items:
        it["p_b"] = it["p"].astype(BF16)
        it["n"] = it["mo"] + _mm(it["p_b"], it["mo"].astype(BF16))
    for it in items:
        it["n_b"] = it["n"].astype(BF16)
        it["n2"] = _mm(it["n_b"], it["n_b"])
    for it in items:
        it["q"] = it["n2"] - it["n"] - _mm(it["n2"].astype(BF16), it["n_b"])
    for it in items:
        it["tinv"] = it["q"] + it["p"] + _mm(it["q"].astype(BF16), it["p_b"])
    for it in items:
        uw = it["rhs"] + _mm(it["tinv"].astype(BF16), it["rhs"].astype(BF16))
        it["uw_b"] = uw.astype(BF16)
    for it in items:
        ktuw = _mm_tn(it["k_dec"], it["uw_b"])
        it["ktu"] = ktuw[:, :Dh]
        it["ktw_b"] = ktuw[:, Dh:].astype(BF16)
    for it in items:
        quw = _mm(it["qk"], it["uw_b"])
        it["o0"] = quw[:, :Dh]
        it["q_eff_b"] = (it["qe"] - quw[:, Dh:]).astype(BF16)

    states = [state_ref[h] for h in range(DN_HEADS)]
    for c in range(nc):
        chunk_items = items[c * DN_HEADS:(c + 1) * DN_HEADS]
        prods = [_mm(jnp.concatenate([it["ktw_b"], it["q_eff_b"]], axis=0), states[it["h"]].astype(BF16))
                 for it in chunk_items]
        for it, prod in zip(chunk_items, prods):
            h, rs = it["h"], it["rs"]
            o = prod[Dh:] + it["o0"]
            states[h] = states[h] * it["g_last"] + it["ktu"] - prod[:Dh]
            zg = z_ref[rs, h * Dh:(h + 1) * Dh].astype(F32)
            o_ref[rs, h * Dh:(h + 1) * Dh] = (_rms_norm(o, onw_ref[...]) * _silu(zg)).astype(o_ref.dtype)
    for h in range(DN_HEADS):
        state_ref[h] = states[h]


def _deltanet(dn_qkv, small, z, a_log, dt_bias, onorm_w, batch, seq):
    T = dn_qkv.shape[0]
    rows = DN_BLOCK_CHUNKS * DN_CHUNK
    nblocks = seq // rows
    lane = jnp.arange(LANES)
    in_decay_lanes = (lane >= DN_HEADS) & (lane < 2 * DN_HEADS)
    idx = jnp.clip(lane - DN_HEADS, 0, DN_HEADS - 1)
    avec = jnp.where(in_decay_lanes, -jnp.exp(a_log.astype(F32))[idx], 0.0).reshape(1, LANES)
    dtb = jnp.where(in_decay_lanes, dt_bias.astype(F32)[idx], 0.0).reshape(1, LANES)
    row = lambda b, c: (b * nblocks + c, 0)
    return pl.pallas_call(
        _deltanet_body,
        grid=(batch, nblocks),
        in_specs=[pl.BlockSpec((rows, 3 * DN_WIDTH), row),
                  pl.BlockSpec((rows, LANES), row),
                  pl.BlockSpec((rows, DN_WIDTH), row),
                  _resident((1, LANES)), _resident((1, LANES)), _resident((1, DN_HEAD_DIM))],
        out_specs=pl.BlockSpec((rows, DN_WIDTH), row),
        out_shape=jax.ShapeDtypeStruct((T, DN_WIDTH), BF16),
        scratch_shapes=[pltpu.VMEM((DN_HEADS, DN_HEAD_DIM, DN_HEAD_DIM), F32)],
        compiler_params=_params("parallel", "arbitrary"),
        name="gated_deltanet",
    )(dn_qkv, small, z, avec, dtb, onorm_w.astype(F32).reshape(1, DN_HEAD_DIM))


def _mixer_output(o_refs, lse_refs, relay_refs, dilations, ob_ref, gates_ref, x_ref, expand_ref, wpa_ref, wpb_ref,
                  wo_ref):
    D = x_ref.shape[-1]
    relay_refs = list(relay_refs)
    o_nat, lses = [], []
    for o_ref, lse_ref, d in zip(o_refs, lse_refs, dilations):
        if d == 1:
            o_nat.append(o_ref[...].astype(F32))
            lses.append(lse_ref[...])
        else:
            o_nat.append(_from_dilated(o_ref, relay_refs.pop(0), d))
            lses.append(_from_dilated(lse_ref, relay_refs.pop(0), d))
    mx = functools.reduce(jnp.maximum, lses)
    es = [jnp.exp(l - mx) for l in lses]
    inv = 1.0 / functools.reduce(lambda a, b: a + b, es)
    o_a = None
    for e, o_g in zip(es, o_nat):
        alpha = e * inv
        hi = alpha.astype(BF16)
        lo = (alpha - hi.astype(F32)).astype(BF16)
        alpha_wide = jnp.dot(jnp.concatenate([hi, lo], axis=1), expand_ref[...], preferred_element_type=F32)
        term = alpha_wide * o_g
        o_a = term if o_a is None else o_a + term
    pa = jnp.dot(o_a.astype(BF16), wpa_ref[...], preferred_element_type=F32)
    pb = jnp.dot(ob_ref[...], wpb_ref[...], preferred_element_type=F32)
    ga = gates_ref[:, :D].astype(F32)
    gb = gates_ref[:, D:].astype(F32)
    y = _sigmoid(ga) * pa + _sigmoid(gb) * pb
    return x_ref[...] + jnp.dot(y.astype(BF16), wo_ref[...], preferred_element_type=F32)


def _mixer_ffn_body(*refs, dilations, tiles_per_seq, col_tile, final_norm):
    ng = len(dilations)
    o_refs = refs[:ng]
    lse_refs = refs[ng:2 * ng]
    (ob_ref, gates_ref, x_ref, expand_ref, wpa_ref, wpb_ref, wo_ref,
     nw_ref, wup_ref, cw_ref, cb_ref, wdown_ref, fw_ref, out_ref, act_ref, hcarry_ref) = refs[2 * ng:2 * ng + 16]
    relay_refs = refs[2 * ng + 16:]
    dff = wdown_ref.shape[0]
    halo_rows = hcarry_ref.shape[0]
    rows = x_ref.shape[0]
    i = pl.program_id(0)

    @pl.when(i % tiles_per_seq == 0)
    def _():
        hcarry_ref[...] = jnp.zeros_like(hcarry_ref)

    h = _mixer_output(o_refs, lse_refs, relay_refs, dilations, ob_ref, gates_ref, x_ref, expand_ref, wpa_ref,
                      wpb_ref, wo_ref)
    left = hcarry_ref[...]
    hcarry_ref[...] = h[rows - halo_rows:]
    hn = _rms_norm(jnp.concatenate([left, h], axis=0), nw_ref[...]).astype(BF16)

    def conv(c0):
        u = jnp.dot(hn, wup_ref[:, c0:c0 + col_tile], preferred_element_type=F32)
        acc = u * cw_ref[FFN_CONV - 1:FFN_CONV, c0:c0 + col_tile]
        for j in range(1, FFN_CONV):
            acc = acc + pltpu.roll(u, j, axis=0) * cw_ref[FFN_CONV - 1 - j:FFN_CONV - j, c0:c0 + col_tile]
        return acc[halo_rows:] + cb_ref[:, c0:c0 + col_tile]

    for c0 in range(0, dff, col_tile):
        act_ref[:, c0:c0 + col_tile] = (_silu(conv(c0)) * conv(dff + c0)).astype(BF16)
    y = h + jnp.dot(act_ref[...], wdown_ref[...], preferred_element_type=F32)
    if final_norm:
        y = _rms_norm(y, fw_ref[...])
    out_ref[...] = y


def _mixer_ffn(o_groups, lse_groups, dilations, o_b, gates, x2d, layer, w_pa, w_pb, w_o, seq, norm_w, w_up, conv_w,
               conv_b, w_down, final_w, final_norm):
    T, D = x2d.shape
    W = ATTN_GROUP_WIDTH
    dff = w_down.shape[1]
    col_tile = 2 * LANES
    lane = jnp.arange(2 * LANES)[:, None] % LANES
    expand = (lane == (jnp.arange(W)[None, :] // ATTN_HEAD_DIM)).astype(BF16)
    row = lambda i: (i, 0)
    tile = lambda n, d=1: pl.BlockSpec((ROW_TILE // d, d * n), row)
    relay = []
    for d in dilations:
        if d > 1:
            relay += [pltpu.VMEM((W // LANES, ROW_TILE, LANES), F32), pltpu.VMEM((1, ROW_TILE, LANES), F32)]
    body = functools.partial(_mixer_ffn_body, dilations=tuple(dilations), tiles_per_seq=seq // ROW_TILE,
                             col_tile=col_tile, final_norm=final_norm)
    stacks = (w_pa, w_pb, w_o, norm_w, w_up, conv_w, conv_b, w_down)
    return pl.pallas_call(
        body,
        grid=(T // ROW_TILE,),
        in_specs=[tile(W, d) for d in dilations] + [tile(LANES, d) for d in dilations]
                 + [tile(DN_WIDTH), tile(2 * D), tile(D), _resident(expand.shape)]
                 + [_layer_resident(w, layer) for w in stacks] + [_resident((1, D))],
        out_specs=tile(D),
        out_shape=jax.ShapeDtypeStruct((T, D), F32),
        scratch_shapes=[pltpu.VMEM((ROW_TILE, dff), BF16), pltpu.VMEM((SUBLANES_F32, D), F32)] + relay,
        compiler_params=_params("arbitrary"),
        name="mixer_out_conv_glu_ffn",
    )(*o_groups, *lse_groups, o_b, gates, x2d, expand, *stacks, final_w.reshape(1, D))


def _split_in_weights(w_in):
    ng = len(ATTN_GROUPS)
    W = ATTN_GROUP_WIDTH
    aw = ng * W
    n_main = 3 * aw + 3 * DN_WIDTH
    n_small = 2 * DN_HEADS
    D = w_in.shape[1]
    main = w_in.astype(BF16)
    small = jnp.pad(w_in[:, :, n_main:n_main + n_small], ((0, 0), (0, 0), (0, LANES - n_small)))
    tail = jnp.concatenate([w_in[:, :, n_main + n_small:], small], axis=2).astype(BF16)
    columns = [(0, part * aw + g * W) for g in range(ng) for part in range(3)]
    columns += [(0, 3 * aw), (1, 0), (1, DN_WIDTH), (1, DN_WIDTH + 2 * D)]
    widths = [W] * (3 * ng) + [3 * DN_WIDTH, DN_WIDTH, 2 * D, LANES]
    dtypes = [BF16] * (3 * ng + 3) + [F32]
    return main, tail, columns, widths, dtypes


def kernel(x, norm1_w, w_in, dn_conv_w, dn_a_log, dn_dt_bias, dn_onorm_w, w_pa, w_pb, w_o, norm2_w, w_up,
           ffn_conv_w, ffn_conv_b, w_down, final_norm_w):
    B, S, D = x.shape
    depth = w_in.shape[0]
    assert S % ROW_TILE == 0 and S % (DN_BLOCK_CHUNKS * DN_CHUNK) == 0
    assert all(ROW_TILE % (SUBLANES_BF16 * d) == 0 for _, d in ATTN_GROUPS)
    xf = x.astype(F32).reshape(B * S, D)
    w_main, w_tail, columns, widths, dtypes = _split_in_weights(w_in)
    row_stack = lambda p: p.astype(F32).reshape(depth, 1, -1)
    norm1_s, norm2_s, conv_b_s = row_stack(norm1_w), row_stack(norm2_w), row_stack(ffn_conv_b)
    dn_conv_s, ffn_conv_s = dn_conv_w.astype(F32), ffn_conv_w.astype(F32)
    w_pa_s, w_pb_s, w_o_s, w_up_s, w_down_s = (w.astype(BF16) for w in (w_pa, w_pb, w_o, w_up, w_down))
    ng = len(ATTN_GROUPS)
    group_dilations = [d for _, d in ATTN_GROUPS]
    out_dilations = [d for d in group_dilations for _ in range(3)] + [1] * (len(widths) - 3 * ng)
    for l in range(depth):
        outs = _in_projection(xf, l, norm1_s, w_main, w_tail, dn_conv_s, widths, dtypes, out_dilations, columns,
                              dn_index=3 * ng, seq=S)
        o_groups, lse_groups = _dilated_attention([outs[3 * g:3 * g + 3] for g in range(ng)], B, S)
        dn_qkv, z, gates, small = outs[3 * ng:]
        o_b = _deltanet(dn_qkv, small, z, dn_a_log[l], dn_dt_bias[l], dn_onorm_w[l], B, S)
        xf = _mixer_ffn(o_groups, lse_groups, group_dilations, o_b, gates, xf, l, w_pa_s, w_pb_s, w_o_s, S, norm2_s,
                        w_up_s, ffn_conv_s, conv_b_s, w_down_s, final_norm_w.astype(F32),
                        final_norm=(l == depth - 1))
    return xf.reshape(B, S, D).astype(x.dtype)
```

```python
import functools

import jax
import jax.numpy as jnp
from jax import lax
from jax.experimental import pallas as pl
from jax.experimental.pallas import tpu as pltpu

F32 = jnp.float32
BF16 = jnp.bfloat16
HIGHEST = lax.Precision.HIGHEST

NORM_EPS = 1e-6

ATTN_HEAD_DIM = 64
ATTN_HEADS_PER_GROUP = 4
ATTN_GROUP_WIDTH = ATTN_HEADS_PER_GROUP * ATTN_HEAD_DIM
ATTN_GROUPS = ((128, 1), (512, 4), (2048, 16))
ATTN_BLOCK = 128
ATTN_STEP_BLOCKS = 8
DN_HEADS = 4
DN_HEAD_DIM = 128
DN_WIDTH = DN_HEADS * DN_HEAD_DIM
DN_CONV = 4
DN_CHUNK = 64
DN_BLOCK_CHUNKS = 8
DN_SOLVE_BLOCK = 16
FFN_CONV = 3

LANES = 128
SUBLANES_F32 = 8
SUBLANES_BF16 = 16
VMEM_LIMIT_BYTES = 56 * 1024 * 1024

ROW_TILE = 512


def _resident(shape):
    nd = len(shape)
    return pl.BlockSpec(shape, lambda *_: (0,) * nd, pipeline_mode=pl.Buffered(1))


def _layer_resident(stacked, layer):
    nd = stacked.ndim - 1
    return pl.BlockSpec((None,) + stacked.shape[1:], lambda *_: (layer,) + (0,) * nd, pipeline_mode=pl.Buffered(1))


def _params(*semantics):
    return pltpu.CompilerParams(dimension_semantics=semantics, vmem_limit_bytes=VMEM_LIMIT_BYTES)


def _rms_norm(x, w):
    return x * lax.rsqrt(jnp.mean(x * x, axis=-1, keepdims=True) + NORM_EPS) * w


def _silu(x):
    return x * (1.0 / (1.0 + jnp.exp(-x)))


def _sigmoid(x):
    return 1.0 / (1.0 + jnp.exp(-x))


def _to_dilated(y, relay_ref, o_ref, d):
    rows = y.shape[0]
    for half in range(ATTN_GROUP_WIDTH // LANES):
        relay_ref[half] = y[:, half * LANES:(half + 1) * LANES]
    for r in range(d):
        for half in range(ATTN_GROUP_WIDTH // LANES):
            c0 = r * ATTN_GROUP_WIDTH + half * LANES
            o_ref[:, c0:c0 + LANES] = relay_ref[half, pl.ds(r, rows // d, stride=d), :].astype(o_ref.dtype)


def _from_dilated(x_ref, relay_ref, d):
    rows = relay_ref.shape[1]
    nslab = relay_ref.shape[0]
    width = nslab * LANES
    for r in range(d):
        for slab in range(nslab):
            c0 = r * width + slab * LANES
            relay_ref[slab, pl.ds(r, rows // d, stride=d), :] = x_ref[:, c0:c0 + LANES].astype(F32)
    return jnp.concatenate([relay_ref[slab] for slab in range(nslab)], axis=1)


def _inproj_body(x_ref, nw_ref, wmain_ref, wtail_ref, cw_ref, *refs, dilations, columns, dn_index, tiles_per_seq):
    n_out = len(dilations)
    out_refs = refs[:n_out]
    ydn_ref = refs[n_out]
    relay_refs = refs[n_out + 1:]
    w_refs = (wmain_ref, wtail_ref)
    i = pl.program_id(0)
    xn = _rms_norm(x_ref[...], nw_ref[...]).astype(BF16)

    halo = ydn_ref.shape[0] - x_ref.shape[0]
    rows = x_ref.shape[0]
    dn_ref = out_refs[dn_index]

    @pl.when(i % tiles_per_seq == 0)
    def _():
        ydn_ref[0:halo, :] = jnp.zeros((halo, ydn_ref.shape[1]), F32)

    @pl.when(i % tiles_per_seq > 0)
    def _():
        ydn_ref[0:halo, :] = ydn_ref[rows:rows + halo, :]

    step = 2 * DN_HEAD_DIM
    dn_w, dn_c0 = columns[dn_index]
    dn_pieces = list(range(0, 3 * DN_WIDTH, step))

    def dn_piece(cb):
        ydn_ref[halo:, cb:cb + step] = jnp.dot(xn, w_refs[dn_w][:, dn_c0 + cb:dn_c0 + cb + step],
                                               preferred_element_type=F32)
        for col in range(cb, cb + step, DN_HEAD_DIM):
            _dn_conv_block(ydn_ref, cw_ref, dn_ref, col, halo)

    others = [(j, p0) for j, (o_ref, d) in enumerate(zip(out_refs, dilations)) if j != dn_index
              for p0 in range(0, o_ref.shape[-1] // d, step)]
    every = max(1, len(others) // (len(dn_pieces) + 1))
    n_relay = 0
    for t, (j, p0) in enumerate(others):
        o_ref, d = out_refs[j], dilations[j]
        n = o_ref.shape[-1] // d
        pn = min(step, n - p0)
        wi, c0 = columns[j]
        y = jnp.dot(xn, w_refs[wi][:, c0 + p0:c0 + p0 + pn], preferred_element_type=F32)
        if d == 1:
            o_ref[:, p0:p0 + pn] = y.astype(o_ref.dtype)
        else:
            _to_dilated(y, relay_refs[n_relay], o_ref, d)
            n_relay += 1
        if dn_pieces and (t + 1) % every == 0:
            dn_piece(dn_pieces.pop(0))
    while dn_pieces:
        dn_piece(dn_pieces.pop(0))


def _dn_conv_block(ydn_ref, cw_ref, o_ref, col, halo):
    Dh = DN_HEAD_DIM
    rows = o_ref.shape[0]
    lanes = slice(col, col + Dh)
    acc = ydn_ref[halo:, lanes] * cw_ref[DN_CONV - 1:DN_CONV, lanes]
    for j in range(1, DN_CONV):
        acc = acc + ydn_ref[halo - j:halo - j + rows, lanes] * cw_ref[DN_CONV - 1 - j:DN_CONV - j, lanes]
    a = _silu(acc)
    if col < 2 * DN_WIDTH:
        a = a * lax.rsqrt(jnp.sum(a * a, axis=-1, keepdims=True) + NORM_EPS)
    if col < DN_WIDTH:
        a = a * (Dh ** -0.5)
    o_ref[:, lanes] = a.astype(o_ref.dtype)


def _in_projection(x2d, layer, norm_w, w_main, w_tail, dn_conv_w, out_widths, out_dtypes, dilations, columns,
                   dn_index, seq):
    T, D = x2d.shape
    grid = (T // ROW_TILE,)
    row = lambda i: (i, 0)
    n_relay = sum(1 for d in dilations if d > 1)
    body = functools.partial(_inproj_body, dilations=tuple(dilations), columns=tuple(columns), dn_index=dn_index,
                             tiles_per_seq=seq // ROW_TILE)
    return pl.pallas_call(
        body,
        grid=grid,
        in_specs=[pl.BlockSpec((ROW_TILE, D), row), _layer_resident(norm_w, layer), _layer_resident(w_main, layer),
                  _layer_resident(w_tail, layer), _layer_resident(dn_conv_w, layer)],
        out_specs=[pl.BlockSpec((ROW_TILE // d, d * n), row) for n, d in zip(out_widths, dilations)],
        out_shape=[jax.ShapeDtypeStruct((T // d, d * n), dt)
                   for n, dt, d in zip(out_widths, out_dtypes, dilations)],
        scratch_shapes=[pltpu.VMEM((SUBLANES_F32 + ROW_TILE, 3 * DN_WIDTH), F32)]
                       + [pltpu.VMEM((ATTN_GROUP_WIDTH // LANES, ROW_TILE, LANES), F32)] * n_relay,
        compiler_params=_params("arbitrary"),
        name="in_projection",
    )(x2d, norm_w, w_main, w_tail, dn_conv_w)


def _attn_body(*refs, spans, blocks_per_seq):
    ng = len(spans)
    blk = ATTN_BLOCK
    W = ATTN_GROUP_WIDTH
    H = ATTN_HEADS_PER_GROUP
    step = pl.program_id(1)
    row = lax.broadcasted_iota(jnp.int32, (blk, 2 * blk), 0)
    col = lax.broadcasted_iota(jnp.int32, (blk, 2 * blk), 1)
    rel = row + blk - col
    lane_head = lax.broadcasted_iota(jnp.int32, (blk, W), 1) // ATTN_HEAD_DIM
    lse_lane = lax.broadcasted_iota(jnp.int32, (blk, LANES), 1)

    units = []
    for g in range(ng):
        q_ref, kp_ref, kc_ref, vp_ref, vc_ref = refs[5 * g:5 * g + 5]
        o_ref, lse_ref = refs[5 * ng + 2 * g:5 * ng + 2 * g + 2]
        band = (rel >= 0) & (rel <= spans[g])
        first_block = step % blocks_per_seq[g] == 0
        band_first = band & (col >= jnp.where(first_block, blk, 0))
        for rr in range(q_ref.shape[1] // W):
            for jb in range(q_ref.shape[0] // blk):
                units.append(dict(q=q_ref, kp=kp_ref, kc=kc_ref, vp=vp_ref, vc=vc_ref, o=o_ref, lse=lse_ref,
                                  rr=rr, jb=jb, lanes=slice(rr * W, (rr + 1) * W),
                                  valid=band_first if jb == 0 else band))

    def window(prev_ref, cur_ref, u):
        jb, lanes = u["jb"], u["lanes"]
        if jb == 0:
            return jnp.concatenate([prev_ref[:, lanes], cur_ref[0:blk, lanes]], axis=0)
        return cur_ref[(jb - 1) * blk:(jb + 1) * blk, lanes]

    for u in units:
        jb = u["jb"]
        q = u["q"][jb * blk:(jb + 1) * blk, u["lanes"]] * (ATTN_HEAD_DIM ** -0.5)
        qs = jnp.concatenate([jnp.where(lane_head == h, q, jnp.zeros_like(q)) for h in range(H)], axis=0)
        u["scores"] = lax.dot_general(qs, window(u["kp"], u["kc"], u), (((1,), (1,)), ((), ())),
                                      preferred_element_type=F32)
    for u in units:
        probs, maxes = [], []
        for h in range(H):
            s = jnp.where(u["valid"], u["scores"][h * blk:(h + 1) * blk], -jnp.inf)
            m = jnp.max(s, axis=-1, keepdims=True)
            probs.append(jnp.exp((s - m).astype(BF16)))
            maxes.append(m)
        u["probs"] = jnp.concatenate(probs, axis=0)
        u["maxes"] = maxes
    ones = jnp.ones((2 * blk, LANES), BF16)
    for u in units:
        jb = u["jb"]
        pv = jnp.dot(u["probs"], window(u["vp"], u["vc"], u), preferred_element_type=F32)
        den = jnp.dot(u["probs"], ones, preferred_element_type=F32)
        o_acc = jnp.zeros((blk, W), F32)
        lse_acc = jnp.zeros((blk, LANES), F32)
        for h in range(H):
            den_h = den[h * blk:(h + 1) * blk]
            rden = 1.0 / den_h
            o_h = pv[h * blk:(h + 1) * blk] * jnp.concatenate([rden] * (W // LANES), axis=1)
            o_acc = jnp.where(lane_head == h, o_h, o_acc)
            lse_acc = jnp.where(lse_lane == h, u["maxes"][h] + jnp.log(den_h), lse_acc)
        u["o"][jb * blk:(jb + 1) * blk, u["lanes"]] = o_acc.astype(u["o"].dtype)
        u["lse"][jb * blk:(jb + 1) * blk, u["rr"] * LANES:(u["rr"] + 1) * LANES] = lse_acc


def _dilated_attention(qkv_groups, batch, seq):
    W = ATTN_GROUP_WIDTH
    operands, in_specs, out_specs, out_shape, spans, blocks_per_seq, steps = [], [], [], [], [], [], None
    for (window, d), (q, k, v) in zip(ATTN_GROUPS, qkv_groups):
        L = seq // d
        nq = min(ATTN_STEP_BLOCKS, L // ATTN_BLOCK)
        nr = min(ATTN_STEP_BLOCKS // nq, d)
        qrows = nq * ATTN_BLOCK
        n_i = L // qrows
        assert L % qrows == 0 and d % nr == 0
        assert steps in (None, (d // nr) * n_i)
        steps = (d // nr) * n_i
        view = lambda t, L=L: t.reshape(batch, L, t.shape[-1])
        cur = lambda b, s, n_i=n_i: (b, s % n_i, s // n_i)
        prev = lambda b, s, n_i=n_i, nq=nq: (b, jnp.maximum((s % n_i) * nq - 1, 0), s // n_i)
        cur_spec = pl.BlockSpec((None, qrows, nr * W), cur)
        prev_spec = pl.BlockSpec((None, ATTN_BLOCK, nr * W), prev)
        operands += [view(q), view(k), view(k), view(v), view(v)]
        in_specs += [cur_spec, prev_spec, cur_spec, prev_spec, cur_spec]
        out_specs += [cur_spec, pl.BlockSpec((None, qrows, nr * LANES), cur)]
        out_shape += [jax.ShapeDtypeStruct((batch, L, d * W), BF16), jax.ShapeDtypeStruct((batch, L, d * LANES), F32)]
        spans.append(window // d)
        blocks_per_seq.append(n_i)
    outs = pl.pallas_call(
        functools.partial(_attn_body, spans=tuple(spans), blocks_per_seq=tuple(blocks_per_seq)),
        grid=(batch, steps),
        in_specs=in_specs,
        out_specs=out_specs,
        out_shape=out_shape,
        compiler_params=_params("parallel", "parallel"),
        name="dilated_attention",
    )(*operands)
    o_groups = [o.reshape(-1, o.shape[-1]) for o in outs[0::2]]
    lse_groups = [l.reshape(-1, l.shape[-1]) for l in outs[1::2]]
    return o_groups, lse_groups


def _mm(a, b):
    return jnp.dot(a, b, preferred_element_type=F32)


def _mm_nt(a, b):
    return lax.dot_general(a, b, (((1,), (1,)), ((), ())), preferred_element_type=F32)


def _mm_tn(a, b):
    return lax.dot_general(a, b, (((0,), (0,)), ((), ())), preferred_element_type=F32)


def _deltanet_body(x_ref, sm_ref, z_ref, avec_ref, dtb_ref, onw_ref, o_ref, state_ref):
    C = DN_CHUNK
    Dh = DN_HEAD_DIM
    rows = x_ref.shape[0]
    blk = pl.program_id(1)

    @pl.when(blk == 0)
    def _():
        state_ref[...] = jnp.zeros_like(state_ref)

    sm = sm_ref[...]
    beta_all = _sigmoid(sm)
    sp_in = sm + dtb_ref[...]
    softplus = jnp.maximum(sp_in, 0.0) + jnp.log(1.0 + jnp.exp(-jnp.abs(sp_in)))
    g_all = avec_ref[...] * softplus
    row_in_chunk = lax.broadcasted_iota(jnp.int32, (rows, LANES), 0) % C
    gc_all = g_all
    s = 1
    while s < C:
        gc_all = gc_all + jnp.where(row_in_chunk >= s, pltpu.roll(gc_all, s, axis=0), 0.0)
        s *= 2
    sel = jnp.where(lax.broadcasted_iota(jnp.int32, (SUBLANES_F32, LANES), 1)
                    == lax.broadcasted_iota(jnp.int32, (SUBLANES_F32, LANES), 0) + DN_HEADS, 1.0, 0.0).astype(F32)

    ri = lax.broadcasted_iota(jnp.int32, (C, C), 0)
    ci = lax.broadcasted_iota(jnp.int32, (C, C), 1)
    incl = ri >= ci
    strict = ri > ci
    same_block = (ri // DN_SOLVE_BLOCK) == (ci // DN_SOLVE_BLOCK)
    nc = rows // C

    items = []
    for c in range(nc):
        rs = slice(c * C, (c + 1) * C)
        gc_c = gc_all[rs]
        gc_rows = lax.dot_general(sel, gc_c, (((1,), (1,)), ((), ())), preferred_element_type=F32,
                                  precision=HIGHEST)
        for h in range(DN_HEADS):
            q_b = x_ref[rs, h * Dh:(h + 1) * Dh]
            k_b = x_ref[rs, (DN_HEADS + h) * Dh:(DN_HEADS + h + 1) * Dh]
            k = k_b.astype(F32)
            v = x_ref[rs, (2 * DN_HEADS + h) * Dh:(2 * DN_HEADS + h + 1) * Dh].astype(F32)
            beta = beta_all[rs, h:h + 1]
            gcol = gc_c[:, DN_HEADS + h:DN_HEADS + h + 1]
            grow = gc_rows[h:h + 1, :]
            decay = jnp.exp(jnp.where(incl, gcol - grow, -jnp.inf))
            kb = k * beta
            eg = jnp.exp(gcol)
            g_last = gcol[C - 1:C, :]
            qk_kk = _mm_nt(jnp.concatenate([q_b, kb.astype(BF16)], axis=0), k_b)
            m = jnp.where(strict, qk_kk[C:] * decay, 0.0)
            md = jnp.where(same_block, m, 0.0)
            items.append(dict(
                rs=rs, h=h, qk=(qk_kk[:C] * decay).astype(BF16), md=md, mo=m - md,
                rhs=jnp.concatenate([v * beta, kb * eg], axis=1),
                qe=q_b.astype(F32) * eg, k_dec=(k * jnp.exp(g_last - gcol)).astype(BF16),
                g_last=jnp.exp(g_last)))

    for it in items:
        md_b = it["md"].astype(BF16)
        it["p"] = -it["md"]
        it["mk"] = _mm(md_b, md_b)
    for step in range(3):
        for it in items:
            mk_b = it["mk"].astype(BF16)
            p_b = it["p"].astype(BF16)
            if step < 2:
                both = _mm(jnp.concatenate([mk_b, p_b], axis=0), mk_b)
                it["p"] = it["p"] + it["mk"] + both[C:]
                it["mk"] = both[:C]
            else:
                it["p"] = it["p"] + it["mk"] + _mm(p_b, mk_b)
    for it in items:
        it["p_b"] = it["p"].astype(BF16)
        it["n"] = it["mo"] + _mm(it["p_b"], it["mo"].astype(BF16))
    for it in items:
        it["n_b"] = it["n"].astype(BF16)
        it["n2"] = _mm(it["n_b"], it["n_b"])
    for it in items:
        it["q"] = it["n2"] - it["n"] - _mm(it["n2"].astype(BF16), it["n_b"])
    for it in items:
        it["tinv"] = it["q"] + it["p"] + _mm(it["q"].astype(BF16), it["p_b"])
    for it in items:
        uw = it["rhs"] + _mm(it["tinv"].astype(BF16), it["rhs"].astype(BF16))
        it["uw_b"] = uw.astype(BF16)
    for it in items:
        ktuw = _mm_tn(it["k_dec"], it["uw_b"])
        it["ktu"] = ktuw[:, :Dh]
        it["ktw_b"] = ktuw[:, Dh:].astype(BF16)
    for it in items:
        quw = _mm(it["qk"], it["uw_b"])
        it["o0"] = quw[:, :Dh]
        it["q_eff_b"] = (it["qe"] - quw[:, Dh:]).astype(BF16)

    states = [state_ref[h] for h in range(DN_HEADS)]
    for c in range(nc):
        chunk_items = items[c * DN_HEADS:(c + 1) * DN_HEADS]
        prods = [_mm(jnp.concatenate([it["ktw_b"], it["q_eff_b"]], axis=0), states[it["h"]].astype(BF16))
                 for it in chunk_items]
        for it, prod in zip(chunk_items, prods):
            h, rs = it["h"], it["rs"]
            o = prod[Dh:] + it["o0"]
            states[h] = states[h] * it["g_last"] + it["ktu"] - prod[:Dh]
            zg = z_ref[rs, h * Dh:(h + 1) * Dh].astype(F32)
            o_ref[rs, h * Dh:(h + 1) * Dh] = (_rms_norm(o, onw_ref[...]) * _silu(zg)).astype(o_ref.dtype)
    for h in range(DN_HEADS):
        state_ref[h] = states[h]


def _deltanet(dn_qkv, small, z, a_log, dt_bias, onorm_w, batch, seq):
    T = dn_qkv.shape[0]
    rows = DN_BLOCK_CHUNKS * DN_CHUNK
    nblocks = seq // rows
    lane = jnp.arange(LANES)
    in_decay_lanes = (lane >= DN_HEADS) & (lane < 2 * DN_HEADS)
    idx = jnp.clip(lane - DN_HEADS, 0, DN_HEADS - 1)
    avec = jnp.where(in_decay_lanes, -jnp.exp(a_log.astype(F32))[idx], 0.0).reshape(1, LANES)
    dtb = jnp.where(in_decay_lanes, dt_bias.astype(F32)[idx], 0.0).reshape(1, LANES)
    row = lambda b, c: (b * nblocks + c, 0)
    return pl.pallas_call(
        _deltanet_body,
        grid=(batch, nblocks),
        in_specs=[pl.BlockSpec((rows, 3 * DN_WIDTH), row),
                  pl.BlockSpec((rows, LANES), row),
                  pl.BlockSpec((rows, DN_WIDTH), row),
                  _resident((1, LANES)), _resident((1, LANES)), _resident((1, DN_HEAD_DIM))],
        out_specs=pl.BlockSpec((rows, DN_WIDTH), row),
        out_shape=jax.ShapeDtypeStruct((T, DN_WIDTH), BF16),
        scratch_shapes=[pltpu.VMEM((DN_HEADS, DN_HEAD_DIM, DN_HEAD_DIM), F32)],
        compiler_params=_params("parallel", "arbitrary"),
        name="gated_deltanet",
    )(dn_qkv, small, z, avec, dtb, onorm_w.astype(F32).reshape(1, DN_HEAD_DIM))


def _mixer_output(o_refs, lse_refs, relay_refs, dilations, ob_ref, gates_ref, x_ref, expand_ref, wpa_ref, wpb_ref,
                  wo_ref):
    D = x_ref.shape[-1]
    relay_refs = list(relay_refs)
    o_nat, lses = [], []
    for o_ref, lse_ref, d in zip(o_refs, lse_refs, dilations):
        if d == 1:
            o_nat.append(o_ref[...].astype(F32))
            lses.append(lse_ref[...])
        else:
            o_nat.append(_from_dilated(o_ref, relay_refs.pop(0), d))
            lses.append(_from_dilated(lse_ref, relay_refs.pop(0), d))
    mx = functools.reduce(jnp.maximum, lses)
    es = [jnp.exp(l - mx) for l in lses]
    inv = 1.0 / functools.reduce(lambda a, b: a + b, es)
    o_a = None
    for e, o_g in zip(es, o_nat):
        alpha = e * inv
        hi = alpha.astype(BF16)
        lo = (alpha - hi.astype(F32)).astype(BF16)
        alpha_wide = jnp.dot(jnp.concatenate([hi, lo], axis=1), expand_ref[...], preferred_element_type=F32)
        term = alpha_wide * o_g
        o_a = term if o_a is None else o_a + term
    pa = jnp.dot(o_a.astype(BF16), wpa_ref[...], preferred_element_type=F32)
    pb = jnp.dot(ob_ref[...], wpb_ref[...], preferred_element_type=F32)
    ga = gates_ref[:, :D].astype(F32)
    gb = gates_ref[:, D:].astype(F32)
    y = _sigmoid(ga) * pa + _sigmoid(gb) * pb
    return x_ref[...] + jnp.dot(y.astype(BF16), wo_ref[...], preferred_element_type=F32)


def _mixer_ffn_body(*refs, dilations, tiles_per_seq, col_tile, final_norm):
    ng = len(dilations)
    o_refs = refs[:ng]
    lse_refs = refs[ng:2 * ng]
    (ob_ref, gates_ref, x_ref, expand_ref, wpa_ref, wpb_ref, wo_ref,
     nw_ref, wup_ref, cw_ref, cb_ref, wdown_ref, fw_ref, out_ref, act_ref, ucarry_ref) = refs[2 * ng:2 * ng + 16]
    relay_refs = refs[2 * ng + 16:]
    dff = wdown_ref.shape[0]
    halo_rows = ucarry_ref.shape[0]
    rows = x_ref.shape[0]
    i = pl.program_id(0)

    @pl.when(i % tiles_per_seq == 0)
    def _():
        ucarry_ref[...] = jnp.zeros_like(ucarry_ref)

    h = _mixer_output(o_refs, lse_refs, relay_refs, dilations, ob_ref, gates_ref, x_ref, expand_ref, wpa_ref,
                      wpb_ref, wo_ref)
    hn = _rms_norm(h, nw_ref[...]).astype(BF16)

    def conv(c0):
        u = jnp.dot(hn, wup_ref[:, c0:c0 + col_tile], preferred_element_type=F32)
        ue = jnp.concatenate([ucarry_ref[:, c0:c0 + col_tile], u], axis=0)
        ucarry_ref[:, c0:c0 + col_tile] = u[rows - halo_rows:]
        acc = ue * cw_ref[FFN_CONV - 1:FFN_CONV, c0:c0 + col_tile]
        for j in range(1, FFN_CONV):
            acc = acc + pltpu.roll(ue, j, axis=0) * cw_ref[FFN_CONV - 1 - j:FFN_CONV - j, c0:c0 + col_tile]
        return acc[halo_rows:] + cb_ref[:, c0:c0 + col_tile]

    for c0 in range(0, dff, col_tile):
        act_ref[:, c0:c0 + col_tile] = (_silu(conv(c0)) * conv(dff + c0)).astype(BF16)
    y = h + jnp.dot(act_ref[...], wdown_ref[...], preferred_element_type=F32)
    if final_norm:
        y = _rms_norm(y, fw_ref[...])
    out_ref[...] = y


def _mixer_ffn(o_groups, lse_groups, dilations, o_b, gates, x2d, layer, w_pa, w_pb, w_o, seq, norm_w, w_up, conv_w,
               conv_b, w_down, final_w, final_norm):
    T, D = x2d.shape
    W = ATTN_GROUP_WIDTH
    dff = w_down.shape[1]
    col_tile = 2 * LANES
    lane = jnp.arange(2 * LANES)[:, None] % LANES
    expand = (lane == (jnp.arange(W)[None, :] // ATTN_HEAD_DIM)).astype(BF16)
    row = lambda i: (i, 0)
    tile = lambda n, d=1: pl.BlockSpec((ROW_TILE // d, d * n), row)
    relay = []
    for d in dilations:
        if d > 1:
            relay += [pltpu.VMEM((W // LANES, ROW_TILE, LANES), F32), pltpu.VMEM((1, ROW_TILE, LANES), F32)]
    body = functools.partial(_mixer_ffn_body, dilations=tuple(dilations), tiles_per_seq=seq // ROW_TILE,
                             col_tile=col_tile, final_norm=final_norm)
    stacks = (w_pa, w_pb, w_o, norm_w, w_up, conv_w, conv_b, w_down)
    return pl.pallas_call(
        body,
        grid=(T // ROW_TILE,),
        in_specs=[tile(W, d) for d in dilations] + [tile(LANES, d) for d in dilations]
                 + [tile(DN_WIDTH), tile(2 * D), tile(D), _resident(expand.shape)]
                 + [_layer_resident(w, layer) for w in stacks] + [_resident((1, D))],
        out_specs=tile(D),
        out_shape=jax.ShapeDtypeStruct((T, D), F32),
        scratch_shapes=[pltpu.VMEM((ROW_TILE, dff), BF16), pltpu.VMEM((SUBLANES_F32, 2 * dff), F32)] + relay,
        compiler_params=_params("arbitrary"),
        name="mixer_out_conv_glu_ffn",
    )(*o_groups, *lse_groups, o_b, gates, x2d, expand, *stacks, final_w.reshape(1, D))


def _split_in_weights(w_in):
    ng = len(ATTN_GROUPS)
    W = ATTN_GROUP_WIDTH
    aw = ng * W
    n_main = 3 * aw + 3 * DN_WIDTH
    n_small = 2 * DN_HEADS
    D = w_in.shape[1]
    main = w_in.astype(BF16)
    small = jnp.pad(w_in[:, :, n_main:n_main + n_small], ((0, 0), (0, 0), (0, LANES - n_small)))
    tail = jnp.concatenate([w_in[:, :, n_main + n_small:], small], axis=2).astype(BF16)
    columns = [(0, part * aw + g * W) for g in range(ng) for part in range(3)]
    columns += [(0, 3 * aw), (1, 0), (1, DN_WIDTH), (1, DN_WIDTH + 2 * D)]
    widths = [W] * (3 * ng) + [3 * DN_WIDTH, DN_WIDTH, 2 * D, LANES]
    dtypes = [BF16] * (3 * ng + 3) + [F32]
    return main, tail, columns, widths, dtypes


def kernel(x, norm1_w, w_in, dn_conv_w, dn_a_log, dn_dt_bias, dn_onorm_w, w_pa, w_pb, w_o, norm2_w, w_up,
           ffn_conv_w, ffn_conv_b, w_down, final_norm_w):
    B, S, D = x.shape
    depth = w_in.shape[0]
    assert S % ROW_TILE == 0 and S % (DN_BLOCK_CHUNKS * DN_CHUNK) == 0
    assert all(ROW_TILE % (SUBLANES_BF16 * d) == 0 for _, d in ATTN_GROUPS)
    xf = x.astype(F32).reshape(B * S, D)
    w_main, w_tail, columns, widths, dtypes = _split_in_weights(w_in)
    row_stack = lambda p: p.astype(F32).reshape(depth, 1, -1)
    norm1_s, norm2_s, conv_b_s = row_stack(norm1_w), row_stack(norm2_w), row_stack(ffn_conv_b)
    dn_conv_s, ffn_conv_s = dn_conv_w.astype(F32), ffn_conv_w.astype(F32)
    w_pa_s, w_pb_s, w_o_s, w_up_s, w_down_s = (w.astype(BF16) for w in (w_pa, w_pb, w_o, w_up, w_down))
    ng = len(ATTN_GROUPS)
    group_dilations = [d for _, d in ATTN_GROUPS]
    out_dilations = [d for d in group_dilations for _ in range(3)] + [1] * (len(widths) - 3 * ng)
    for l in range(depth):
        outs = _in_projection(xf, l, norm1_s, w_main, w_tail, dn_conv_s, widths, dtypes, out_dilations, columns,
                              dn_index=3 * ng, seq=S)
        o_groups, lse_groups = _dilated_attention([outs[3 * g:3 * g + 3] for g in range(ng)], B, S)
        dn_qkv, z, gates, small = outs[3 * ng:]
        o_b = _deltanet(dn_qkv, small, z, dn_a_log[l], dn_dt_bias[l], dn_onorm_w[l], B, S)
        xf = _mixer_ffn(o_groups, lse_groups, group_dilations, o_b, gates, xf, l, w_pa_s, w_pb_s, w_o_s, S, norm2_s,
                        w_up_s, ffn_conv_s, conv_b_s, w_down_s, final_norm_w.astype(F32),
                        final_norm=(l == depth - 1))
    return xf.reshape(B, S, D).astype(x.dtype)
```

```python
import functools

import jax
import jax.numpy as jnp
from jax import lax
from jax.experimental import pallas as pl
from jax.experimental.pallas import tpu as pltpu

F32 = jnp.float32
BF16 = jnp.bfloat16
HIGHEST = lax.Precision.HIGHEST

NORM_EPS = 1e-6

ATTN_HEAD_DIM = 64
ATTN_HEADS_PER_GROUP = 4
ATTN_GROUP_WIDTH = ATTN_HEADS_PER_GROUP * ATTN_HEAD_DIM
ATTN_GROUPS = ((128, 1), (512, 4), (2048, 16))
ATTN_BLOCK = 128
ATTN_STEP_BLOCKS = 8
DN_HEADS = 4
DN_HEAD_DIM = 128
DN_WIDTH = DN_HEADS * DN_HEAD_DIM
DN_CONV = 4
DN_CHUNK = 64
DN_BLOCK_CHUNKS = 8
DN_SOLVE_BLOCK = 16
FFN_CONV = 3

LANES = 128
SUBLANES_F32 = 8
SUBLANES_BF16 = 16
VMEM_LIMIT_BYTES = 56 * 1024 * 1024

ROW_TILE = 512
FFN_COL_TILE = 6 * LANES


def _resident(shape):
    nd = len(shape)
    return pl.BlockSpec(shape, lambda *_: (0,) * nd, pipeline_mode=pl.Buffered(1))


def _layer_resident(stacked, layer):
    nd = stacked.ndim - 1
    return pl.BlockSpec((None,) + stacked.shape[1:], lambda *_: (layer,) + (0,) * nd, pipeline_mode=pl.Buffered(1))


def _params(*semantics):
    return pltpu.CompilerParams(dimension_semantics=semantics, vmem_limit_bytes=VMEM_LIMIT_BYTES)


def _rms_norm(x, w):
    return x * lax.rsqrt(jnp.mean(x * x, axis=-1, keepdims=True) + NORM_EPS) * w


def _silu(x):
    return x * (1.0 / (1.0 + jnp.exp(-x)))


def _sigmoid(x):
    return 1.0 / (1.0 + jnp.exp(-x))


def _to_dilated(y, relay_ref, o_ref, d):
    rows = y.shape[0]
    for half in range(ATTN_GROUP_WIDTH // LANES):
        relay_ref[half] = y[:, half * LANES:(half + 1) * LANES]
    for r in range(d):
        for half in range(ATTN_GROUP_WIDTH // LANES):
            c0 = r * ATTN_GROUP_WIDTH + half * LANES
            o_ref[:, c0:c0 + LANES] = relay_ref[half, pl.ds(r, rows // d, stride=d), :].astype(o_ref.dtype)


def _from_dilated(x_ref, relay_ref, d):
    rows = relay_ref.shape[1]
    nslab = relay_ref.shape[0]
    width = nslab * LANES
    for r in range(d):
        for slab in range(nslab):
            c0 = r * width + slab * LANES
            relay_ref[slab, pl.ds(r, rows // d, stride=d), :] = x_ref[:, c0:c0 + LANES].astype(F32)
    return jnp.concatenate([relay_ref[slab] for slab in range(nslab)], axis=1)


def _inproj_body(x_ref, nw_ref, wmain_ref, wtail_ref, cw_ref, *refs, dilations, columns, dn_index, tiles_per_seq):
    n_out = len(dilations)
    out_refs = refs[:n_out]
    ydn_ref = refs[n_out]
    relay_refs = refs[n_out + 1:]
    w_refs = (wmain_ref, wtail_ref)
    i = pl.program_id(0)
    xn = _rms_norm(x_ref[...], nw_ref[...]).astype(BF16)

    halo = ydn_ref.shape[0] - x_ref.shape[0]
    rows = x_ref.shape[0]
    dn_ref = out_refs[dn_index]

    @pl.when(i % tiles_per_seq == 0)
    def _():
        ydn_ref[0:halo, :] = jnp.zeros((halo, ydn_ref.shape[1]), F32)

    @pl.when(i % tiles_per_seq > 0)
    def _():
        ydn_ref[0:halo, :] = ydn_ref[rows:rows + halo, :]

    step = 2 * DN_HEAD_DIM
    dn_w, dn_c0 = columns[dn_index]
    dn_pieces = list(range(0, 3 * DN_WIDTH, step))

    def dn_piece(cb):
        ydn_ref[halo:, cb:cb + step] = jnp.dot(xn, w_refs[dn_w][:, dn_c0 + cb:dn_c0 + cb + step],
                                               preferred_element_type=F32)
        for col in range(cb, cb + step, DN_HEAD_DIM):
            _dn_conv_block(ydn_ref, cw_ref, dn_ref, col, halo)

    others = [(j, p0) for j, (o_ref, d) in enumerate(zip(out_refs, dilations)) if j != dn_index
              for p0 in range(0, o_ref.shape[-1] // d, step)]
    every = max(1, len(others) // (len(dn_pieces) + 1))
    n_relay = 0
    for t, (j, p0) in enumerate(others):
        o_ref, d = out_refs[j], dilations[j]
        n = o_ref.shape[-1] // d
        pn = min(step, n - p0)
        wi, c0 = columns[j]
        y = jnp.dot(xn, w_refs[wi][:, c0 + p0:c0 + p0 + pn], preferred_element_type=F32)
        if d == 1:
            o_ref[:, p0:p0 + pn] = y.astype(o_ref.dtype)
        else:
            _to_dilated(y, relay_refs[n_relay], o_ref, d)
            n_relay += 1
        if dn_pieces and (t + 1) % every == 0:
            dn_piece(dn_pieces.pop(0))
    while dn_pieces:
        dn_piece(dn_pieces.pop(0))


def _dn_conv_block(ydn_ref, cw_ref, o_ref, col, halo):
    Dh = DN_HEAD_DIM
    rows = o_ref.shape[0]
    lanes = slice(col, col + Dh)
    acc = ydn_ref[halo:, lanes] * cw_ref[DN_CONV - 1:DN_CONV, lanes]
    for j in range(1, DN_CONV):
        acc = acc + ydn_ref[halo - j:halo - j + rows, lanes] * cw_ref[DN_CONV - 1 - j:DN_CONV - j, lanes]
    a = _silu(acc)
    if col < 2 * DN_WIDTH:
        a = a * lax.rsqrt(jnp.sum(a * a, axis=-1, keepdims=True) + NORM_EPS)
    if col < DN_WIDTH:
        a = a * (Dh ** -0.5)
    o_ref[:, lanes] = a.astype(o_ref.dtype)


def _in_projection(x2d, layer, norm_w, w_main, w_tail, dn_conv_w, out_widths, out_dtypes, dilations, columns,
                   dn_index, seq):
    T, D = x2d.shape
    grid = (T // ROW_TILE,)
    row = lambda i: (i, 0)
    n_relay = sum(1 for d in dilations if d > 1)
    body = functools.partial(_inproj_body, dilations=tuple(dilations), columns=tuple(columns), dn_index=dn_index,
                             tiles_per_seq=seq // ROW_TILE)
    return pl.pallas_call(
        body,
        grid=grid,
        in_specs=[pl.BlockSpec((ROW_TILE, D), row), _layer_resident(norm_w, layer), _layer_resident(w_main, layer),
                  _layer_resident(w_tail, layer), _layer_resident(dn_conv_w, layer)],
        out_specs=[pl.BlockSpec((ROW_TILE // d, d * n), row) for n, d in zip(out_widths, dilations)],
        out_shape=[jax.ShapeDtypeStruct((T // d, d * n), dt)
                   for n, dt, d in zip(out_widths, out_dtypes, dilations)],
        scratch_shapes=[pltpu.VMEM((SUBLANES_F32 + ROW_TILE, 3 * DN_WIDTH), F32)]
                       + [pltpu.VMEM((ATTN_GROUP_WIDTH // LANES, ROW_TILE, LANES), F32)] * n_relay,
        compiler_params=_params("arbitrary"),
        name="in_projection",
    )(x2d, norm_w, w_main, w_tail, dn_conv_w)


def _attn_body(*refs, spans, blocks_per_seq):
    ng = len(spans)
    blk = ATTN_BLOCK
    W = ATTN_GROUP_WIDTH
    H = ATTN_HEADS_PER_GROUP
    step = pl.program_id(1)
    row = lax.broadcasted_iota(jnp.int32, (blk, 2 * blk), 0)
    col = lax.broadcasted_iota(jnp.int32, (blk, 2 * blk), 1)
    rel = row + blk - col
    lane_head = lax.broadcasted_iota(jnp.int32, (blk, W), 1) // ATTN_HEAD_DIM
    lse_lane = lax.broadcasted_iota(jnp.int32, (blk, LANES), 1)

    units = []
    for g in range(ng):
        q_ref, kp_ref, kc_ref, vp_ref, vc_ref = refs[5 * g:5 * g + 5]
        o_ref, lse_ref = refs[5 * ng + 2 * g:5 * ng + 2 * g + 2]
        band = (rel >= 0) & (rel <= spans[g])
        first_block = step % blocks_per_seq[g] == 0
        band_first = band & (col >= jnp.where(first_block, blk, 0))
        for rr in range(q_ref.shape[1] // W):
            for jb in range(q_ref.shape[0] // blk):
                units.append(dict(q=q_ref, kp=kp_ref, kc=kc_ref, vp=vp_ref, vc=vc_ref, o=o_ref, lse=lse_ref,
                                  rr=rr, jb=jb, lanes=slice(rr * W, (rr + 1) * W),
                                  valid=band_first if jb == 0 else band))

    def window(prev_ref, cur_ref, u):
        jb, lanes = u["jb"], u["lanes"]
        if jb == 0:
            return jnp.concatenate([prev_ref[:, lanes], cur_ref[0:blk, lanes]], axis=0)
        return cur_ref[(jb - 1) * blk:(jb + 1) * blk, lanes]

    for u in units:
        jb = u["jb"]
        q = u["q"][jb * blk:(jb + 1) * blk, u["lanes"]] * (ATTN_HEAD_DIM ** -0.5)
        qs = jnp.concatenate([jnp.where(lane_head == h, q, jnp.zeros_like(q)) for h in range(H)], axis=0)
        u["scores"] = lax.dot_general(qs, window(u["kp"], u["kc"], u), (((1,), (1,)), ((), ())),
                                      preferred_element_type=F32)
    for u in units:
        probs, maxes = [], []
        for h in range(H):
            s = jnp.where(u["valid"], u["scores"][h * blk:(h + 1) * blk], -jnp.inf)
            m = jnp.max(s, axis=-1, keepdims=True)
            probs.append(jnp.exp((s - m).astype(BF16)))
            maxes.append(m)
        u["probs"] = jnp.concatenate(probs, axis=0)
        u["maxes"] = maxes
    ones = jnp.ones((2 * blk, LANES), BF16)
    for u in units:
        jb = u["jb"]
        pv = jnp.dot(u["probs"], window(u["vp"], u["vc"], u), preferred_element_type=F32)
        den = jnp.dot(u["probs"], ones, preferred_element_type=F32)
        o_acc = jnp.zeros((blk, W), F32)
        lse_acc = jnp.zeros((blk, LANES), F32)
        for h in range(H):
            den_h = den[h * blk:(h + 1) * blk]
            rden = 1.0 / den_h
            o_h = pv[h * blk:(h + 1) * blk] * jnp.concatenate([rden] * (W // LANES), axis=1)
            o_acc = jnp.where(lane_head == h, o_h, o_acc)
            lse_acc = jnp.where(lse_lane == h, u["maxes"][h] + jnp.log(den_h), lse_acc)
        u["o"][jb * blk:(jb + 1) * blk, u["lanes"]] = o_acc.astype(u["o"].dtype)
        u["lse"][jb * blk:(jb + 1) * blk, u["rr"] * LANES:(u["rr"] + 1) * LANES] = lse_acc


def _dilated_attention(qkv_groups, batch, seq):
    W = ATTN_GROUP_WIDTH
    operands, in_specs, out_specs, out_shape, spans, blocks_per_seq, steps = [], [], [], [], [], [], None
    for (window, d), (q, k, v) in zip(ATTN_GROUPS, qkv_groups):
        L = seq // d
        nq = min(ATTN_STEP_BLOCKS, L // ATTN_BLOCK)
        nr = min(ATTN_STEP_BLOCKS // nq, d)
        qrows = nq * ATTN_BLOCK
        n_i = L // qrows
        assert L % qrows == 0 and d % nr == 0
        assert steps in (None, (d // nr) * n_i)
        steps = (d // nr) * n_i
        view = lambda t, L=L: t.reshape(batch, L, t.shape[-1])
        cur = lambda b, s, n_i=n_i: (b, s % n_i, s // n_i)
        prev = lambda b, s, n_i=n_i, nq=nq: (b, jnp.maximum((s % n_i) * nq - 1, 0), s // n_i)
        cur_spec = pl.BlockSpec((None, qrows, nr * W), cur)
        prev_spec = pl.BlockSpec((None, ATTN_BLOCK, nr * W), prev)
        operands += [view(q), view(k), view(k), view(v), view(v)]
        in_specs += [cur_spec, prev_spec, cur_spec, prev_spec, cur_spec]
        out_specs += [cur_spec, pl.BlockSpec((None, qrows, nr * LANES), cur)]
        out_shape += [jax.ShapeDtypeStruct((batch, L, d * W), BF16), jax.ShapeDtypeStruct((batch, L, d * LANES), F32)]
        spans.append(window // d)
        blocks_per_seq.append(n_i)
    outs = pl.pallas_call(
        functools.partial(_attn_body, spans=tuple(spans), blocks_per_seq=tuple(blocks_per_seq)),
        grid=(batch, steps),
        in_specs=in_specs,
        out_specs=out_specs,
        out_shape=out_shape,
        compiler_params=_params("parallel", "parallel"),
        name="dilated_attention",
    )(*operands)
    o_groups = [o.reshape(-1, o.shape[-1]) for o in outs[0::2]]
    lse_groups = [l.reshape(-1, l.shape[-1]) for l in outs[1::2]]
    return o_groups, lse_groups


def _mm(a, b):
    return jnp.dot(a, b, preferred_element_type=F32)


def _mm_nt(a, b):
    return lax.dot_general(a, b, (((1,), (1,)), ((), ())), preferred_element_type=F32)


def _mm_tn(a, b):
    return lax.dot_general(a, b, (((0,), (0,)), ((), ())), preferred_element_type=F32)


def _deltanet_body(x_ref, sm_ref, z_ref, avec_ref, dtb_ref, onw_ref, o_ref, state_ref):
    C = DN_CHUNK
    Dh = DN_HEAD_DIM
    rows = x_ref.shape[0]
    blk = pl.program_id(1)

    @pl.when(blk == 0)
    def _():
        state_ref[...] = jnp.zeros_like(state_ref)

    sm = sm_ref[...]
    beta_all = _sigmoid(sm)
    sp_in = sm + dtb_ref[...]
    softplus = jnp.maximum(sp_in, 0.0) + jnp.log(1.0 + jnp.exp(-jnp.abs(sp_in)))
    g_all = avec_ref[...] * softplus
    row_in_chunk = lax.broadcasted_iota(jnp.int32, (rows, LANES), 0) % C
    gc_all = g_all
    s = 1
    while s < C:
        gc_all = gc_all + jnp.where(row_in_chunk >= s, pltpu.roll(gc_all, s, axis=0), 0.0)
        s *= 2
    sel = jnp.where(lax.broadcasted_iota(jnp.int32, (SUBLANES_F32, LANES), 1)
                    == lax.broadcasted_iota(jnp.int32, (SUBLANES_F32, LANES), 0) + DN_HEADS, 1.0, 0.0).astype(F32)

    ri = lax.broadcasted_iota(jnp.int32, (C, C), 0)
    ci = lax.broadcasted_iota(jnp.int32, (C, C), 1)
    incl = ri >= ci
    strict = ri > ci
    same_block = (ri // DN_SOLVE_BLOCK) == (ci // DN_SOLVE_BLOCK)
    nc = rows // C

    items = []
    for c in range(nc):
        rs = slice(c * C, (c + 1) * C)
        gc_c = gc_all[rs]
        gc_rows = lax.dot_general(sel, gc_c, (((1,), (1,)), ((), ())), preferred_element_type=F32,
                                  precision=HIGHEST)
        for h in range(DN_HEADS):
            q_b = x_ref[rs, h * Dh:(h + 1) * Dh]
            k_b = x_ref[rs, (DN_HEADS + h) * Dh:(DN_HEADS + h + 1) * Dh]
            k = k_b.astype(F32)
            v = x_ref[rs, (2 * DN_HEADS + h) * Dh:(2 * DN_HEADS + h + 1) * Dh].astype(F32)
            beta = beta_all[rs, h:h + 1]
            gcol = gc_c[:, DN_HEADS + h:DN_HEADS + h + 1]
            grow = gc_rows[h:h + 1, :]
            decay = jnp.exp(jnp.where(incl, gcol - grow, -jnp.inf))
            kb = k * beta
            eg = jnp.exp(gcol)
            g_last = gcol[C - 1:C, :]
            qk_kk = _mm_nt(jnp.concatenate([q_b, kb.astype(BF16)], axis=0), k_b)
            m = jnp.where(strict, qk_kk[C:] * decay, 0.0)
            md = jnp.where(same_block, m, 0.0)
            items.append(dict(
                rs=rs, h=h, qk=(qk_kk[:C] * decay).astype(BF16), md=md, mo=m - md,
                rhs=jnp.concatenate([v * beta, kb * eg], axis=1),
                qe=q_b.astype(F32) * eg, k_dec=(k * jnp.exp(g_last - gcol)).astype(BF16),
                g_last=jnp.exp(g_last)))

    for it in items:
        md_b = it["md"].astype(BF16)
        it["p"] = -it["md"]
        it["mk"] = _mm(md_b, md_b)
    for step in range(3):
        for it in items:
            mk_b = it["mk"].astype(BF16)
            p_b = it["p"].astype(BF16)
            if step < 2:
                both = _mm(jnp.concatenate([mk_b, p_b], axis=0), mk_b)
                it["p"] = it["p"] + it["mk"] + both[C:]
                it["mk"] = both[:C]
            else:
                it["p"] = it["p"] + it["mk"] + _mm(p_b, mk_b)
    for it in items:
        it["p_b"] = it["p"].astype(BF16)
        it["n"] = it["mo"] + _mm(it["p_b"], it["mo"].astype(BF16))
    for it in items:
        it["n_b"] = it["n"].astype(BF16)
        it["n2"] = _mm(it["n_b"], it["n_b"])
    for it in items:
        it["q"] = it["n2"] - it["n"] - _mm(it["n2"].astype(BF16), it["n_b"])
    for it in items:
        it["tinv"] = it["q"] + it["p"] + _mm(it["q"].astype(BF16), it["p_b"])
    for it in items:
        uw = it["rhs"] + _mm(it["tinv"].astype(BF16), it["rhs"].astype(BF16))
        it["uw_b"] = uw.astype(BF16)
    for it in items:
        ktuw = _mm_tn(it["k_dec"], it["uw_b"])
        it["ktu"] = ktuw[:, :Dh]
        it["ktw_b"] = ktuw[:, Dh:].astype(BF16)
    for it in items:
        quw = _mm(it["qk"], it["uw_b"])
        it["o0"] = quw[:, :Dh]
        it["q_eff_b"] = (it["qe"] - quw[:, Dh:]).astype(BF16)

    states = [state_ref[h] for h in range(DN_HEADS)]
    for c in range(nc):
        chunk_items = items[c * DN_HEADS:(c + 1) * DN_HEADS]
        prods = [_mm(jnp.concatenate([it["ktw_b"], it["q_eff_b"]], axis=0), states[it["h"]].astype(BF16))
                 for it in chunk_items]
        for it, prod in zip(chunk_items, prods):
            h, rs = it["h"], it["rs"]
            o = prod[Dh:] + it["o0"]
            states[h] = states[h] * it["g_last"] + it["ktu"] - prod[:Dh]
            zg = z_ref[rs, h * Dh:(h + 1) * Dh].astype(F32)
            o_ref[rs, h * Dh:(h + 1) * Dh] = (_rms_norm(o, onw_ref[...]) * _silu(zg)).astype(o_ref.dtype)
    for h in range(DN_HEADS):
        state_ref[h] = states[h]


def _deltanet(dn_qkv, small, z, a_log, dt_bias, onorm_w, batch, seq):
    T = dn_qkv.shape[0]
    rows = DN_BLOCK_CHUNKS * DN_CHUNK
    nblocks = seq // rows
    lane = jnp.arange(LANES)
    in_decay_lanes = (lane >= DN_HEADS) & (lane < 2 * DN_HEADS)
    idx = jnp.clip(lane - DN_HEADS, 0, DN_HEADS - 1)
    avec = jnp.where(in_decay_lanes, -jnp.exp(a_log.astype(F32))[idx], 0.0).reshape(1, LANES)
    dtb = jnp.where(in_decay_lanes, dt_bias.astype(F32)[idx], 0.0).reshape(1, LANES)
    row = lambda b, c: (b * nblocks + c, 0)
    return pl.pallas_call(
        _deltanet_body,
        grid=(batch, nblocks),
        in_specs=[pl.BlockSpec((rows, 3 * DN_WIDTH), row),
                  pl.BlockSpec((rows, LANES), row),
                  pl.BlockSpec((rows, DN_WIDTH), row),
                  _resident((1, LANES)), _resident((1, LANES)), _resident((1, DN_HEAD_DIM))],
        out_specs=pl.BlockSpec((rows, DN_WIDTH), row),
        out_shape=jax.ShapeDtypeStruct((T, DN_WIDTH), BF16),
        scratch_shapes=[pltpu.VMEM((DN_HEADS, DN_HEAD_DIM, DN_HEAD_DIM), F32)],
        compiler_params=_params("parallel", "arbitrary"),
        name="gated_deltanet",
    )(dn_qkv, small, z, avec, dtb, onorm_w.astype(F32).reshape(1, DN_HEAD_DIM))


def _mixer_output(o_refs, lse_refs, relay_refs, dilations, ob_ref, gates_ref, x_ref, expand_ref, wpa_ref, wpb_ref,
                  wo_ref):
    D = x_ref.shape[-1]
    relay_refs = list(relay_refs)
    o_nat, lses = [], []
    for o_ref, lse_ref, d in zip(o_refs, lse_refs, dilations):
        if d == 1:
            o_nat.append(o_ref[...].astype(F32))
            lses.append(lse_ref[...])
        else:
            o_nat.append(_from_dilated(o_ref, relay_refs.pop(0), d))
            lses.append(_from_dilated(lse_ref, relay_refs.pop(0), d))
    mx = functools.reduce(jnp.maximum, lses)
    es = [jnp.exp(l - mx) for l in lses]
    inv = 1.0 / functools.reduce(lambda a, b: a + b, es)
    o_a = None
    for e, o_g in zip(es, o_nat):
        alpha = e * inv
        hi = alpha.astype(BF16)
        lo = (alpha - hi.astype(F32)).astype(BF16)
        alpha_wide = jnp.dot(jnp.concatenate([hi, lo], axis=1), expand_ref[...], preferred_element_type=F32)
        term = alpha_wide * o_g
        o_a = term if o_a is None else o_a + term
    pa = jnp.dot(o_a.astype(BF16), wpa_ref[...], preferred_element_type=F32)
    pb = jnp.dot(ob_ref[...], wpb_ref[...], preferred_element_type=F32)
    ga = gates_ref[:, :D].astype(F32)
    gb = gates_ref[:, D:].astype(F32)
    y = _sigmoid(ga) * pa + _sigmoid(gb) * pb
    return x_ref[...] + jnp.dot(y.astype(BF16), wo_ref[...], preferred_element_type=F32)


def _mixer_ffn_body(*refs, dilations, tiles_per_seq, col_tile, final_norm):
    ng = len(dilations)
    o_refs = refs[:ng]
    lse_refs = refs[ng:2 * ng]
    (ob_ref, gates_ref, x_ref, expand_ref, wpa_ref, wpb_ref, wo_ref,
     nw_ref, wup_ref, cw_ref, cb_ref, wdown_ref, fw_ref, out_ref, act_ref, ucarry_ref) = refs[2 * ng:2 * ng + 16]
    relay_refs = refs[2 * ng + 16:]
    dff = wdown_ref.shape[0]
    halo_rows = ucarry_ref.shape[0]
    rows = x_ref.shape[0]
    i = pl.program_id(0)

    @pl.when(i % tiles_per_seq == 0)
    def _():
        ucarry_ref[...] = jnp.zeros_like(ucarry_ref)

    h = _mixer_output(o_refs, lse_refs, relay_refs, dilations, ob_ref, gates_ref, x_ref, expand_ref, wpa_ref,
                      wpb_ref, wo_ref)
    hn = _rms_norm(h, nw_ref[...]).astype(BF16)

    def conv(c0, ct):
        u = jnp.dot(hn, wup_ref[:, c0:c0 + ct], preferred_element_type=F32)
        ue = jnp.concatenate([ucarry_ref[:, c0:c0 + ct], u], axis=0)
        ucarry_ref[:, c0:c0 + ct] = u[rows - halo_rows:]
        acc = ue * cw_ref[FFN_CONV - 1:FFN_CONV, c0:c0 + ct]
        for j in range(1, FFN_CONV):
            acc = acc + pltpu.roll(ue, j, axis=0) * cw_ref[FFN_CONV - 1 - j:FFN_CONV - j, c0:c0 + ct]
        return acc[halo_rows:] + cb_ref[:, c0:c0 + ct]

    for c0 in range(0, dff, col_tile):
        ct = min(col_tile, dff - c0)
        act_ref[:, c0:c0 + ct] = (_silu(conv(c0, ct)) * conv(dff + c0, ct)).astype(BF16)
    y = h + jnp.dot(act_ref[...], wdown_ref[...], preferred_element_type=F32)
    if final_norm:
        y = _rms_norm(y, fw_ref[...])
    out_ref[...] = y


def _mixer_ffn(o_groups, lse_groups, dilations, o_b, gates, x2d, layer, w_pa, w_pb, w_o, seq, norm_w, w_up, conv_w,
               conv_b, w_down, final_w, final_norm):
    T, D = x2d.shape
    W = ATTN_GROUP_WIDTH
    dff = w_down.shape[1]
    col_tile = FFN_COL_TILE
    lane = jnp.arange(2 * LANES)[:, None] % LANES
    expand = (lane == (jnp.arange(W)[None, :] // ATTN_HEAD_DIM)).astype(BF16)
    row = lambda i: (i, 0)
    tile = lambda n, d=1: pl.BlockSpec((ROW_TILE // d, d * n), row)
    relay = []
    for d in dilations:
        if d > 1:
            relay += [pltpu.VMEM((W // LANES, ROW_TILE, LANES), F32), pltpu.VMEM((1, ROW_TILE, LANES), F32)]
    body = functools.partial(_mixer_ffn_body, dilations=tuple(dilations), tiles_per_seq=seq // ROW_TILE,
                             col_tile=col_tile, final_norm=final_norm)
    stacks = (w_pa, w_pb, w_o, norm_w, w_up, conv_w, conv_b, w_down)
    return pl.pallas_call(
        body,
        grid=(T // ROW_TILE,),
        in_specs=[tile(W, d) for d in dilations] + [tile(LANES, d) for d in dilations]
                 + [tile(DN_WIDTH), tile(2 * D), tile(D), _resident(expand.shape)]
                 + [_layer_resident(w, layer) for w in stacks] + [_resident((1, D))],
        out_specs=tile(D),
        out_shape=jax.ShapeDtypeStruct((T, D), F32),
        scratch_shapes=[pltpu.VMEM((ROW_TILE, dff), BF16), pltpu.VMEM((SUBLANES_F32, 2 * dff), F32)] + relay,
        compiler_params=_params("arbitrary"),
        name="mixer_out_conv_glu_ffn",
    )(*o_groups, *lse_groups, o_b, gates, x2d, expand, *stacks, final_w.reshape(1, D))


def _split_in_weights(w_in):
    ng = len(ATTN_GROUPS)
    W = ATTN_GROUP_WIDTH
    aw = ng * W
    n_main = 3 * aw + 3 * DN_WIDTH
    n_small = 2 * DN_HEADS
    D = w_in.shape[1]
    main = w_in.astype(BF16)
    small = jnp.pad(w_in[:, :, n_main:n_main + n_small], ((0, 0), (0, 0), (0, LANES - n_small)))
    tail = jnp.concatenate([w_in[:, :, n_main + n_small:], small], axis=2).astype(BF16)
    columns = [(0, part * aw + g * W) for g in range(ng) for part in range(3)]
    columns += [(0, 3 * aw), (1, 0), (1, DN_WIDTH), (1, DN_WIDTH + 2 * D)]
    widths = [W] * (3 * ng) + [3 * DN_WIDTH, DN_WIDTH, 2 * D, LANES]
    dtypes = [BF16] * (3 * ng + 3) + [F32]
    return main, tail, columns, widths, dtypes


def kernel(x, norm1_w, w_in, dn_conv_w, dn_a_log, dn_dt_bias, dn_onorm_w, w_pa, w_pb, w_o, norm2_w, w_up,
           ffn_conv_w, ffn_conv_b, w_down, final_norm_w):
    B, S, D = x.shape
    depth = w_in.shape[0]
    assert S % ROW_TILE == 0 and S % (DN_BLOCK_CHUNKS * DN_CHUNK) == 0
    assert all(ROW_TILE % (SUBLANES_BF16 * d) == 0 for _, d in ATTN_GROUPS)
    xf = x.astype(F32).reshape(B * S, D)
    w_main, w_tail, columns, widths, dtypes = _split_in_weights(w_in)
    row_stack = lambda p: p.astype(F32).reshape(depth, 1, -1)
    norm1_s, norm2_s, conv_b_s = row_stack(norm1_w), row_stack(norm2_w), row_stack(ffn_conv_b)
    dn_conv_s, ffn_conv_s = dn_conv_w.astype(F32), ffn_conv_w.astype(F32)
    w_pa_s, w_pb_s, w_o_s, w_up_s, w_down_s = (w.astype(BF16) for w in (w_pa, w_pb, w_o, w_up, w_down))
    ng = len(ATTN_GROUPS)
    group_dilations = [d for _, d in ATTN_GROUPS]
    out_dilations = [d for d in group_dilations for _ in range(3)] + [1] * (len(widths) - 3 * ng)
    for l in range(depth):
        outs = _in_projection(xf, l, norm1_s, w_main, w_tail, dn_conv_s, widths, dtypes, out_dilations, columns,
                              dn_index=3 * ng, seq=S)
        o_groups, lse_groups = _dilated_attention([outs[3 * g:3 * g + 3] for g in range(ng)], B, S)
        dn_qkv, z, gates, small = outs[3 * ng:]
        o_b = _deltanet(dn_qkv, small, z, dn_a_log[l], dn_dt_bias[l], dn_onorm_w[l], B, S)
        xf = _mixer_ffn(o_groups, lse_groups, group_dilations, o_b, gates, xf, l, w_pa_s, w_pb_s, w_o_s, S, norm2_s,
                        w_up_s, ffn_conv_s, conv_b_s, w_down_s, final_norm_w.astype(F32),
                        final_norm=(l == depth - 1))
    return xf.reshape(B, S, D).astype(x.dtype)
```

```python
import functools

import jax
import jax.numpy as jnp
from jax import lax
from jax.experimental import pallas as pl
from jax.experimental.pallas import tpu as pltpu

F32 = jnp.float32
BF16 = jnp.bfloat16
HIGHEST = lax.Precision.HIGHEST

NORM_EPS = 1e-6

ATTN_HEAD_DIM = 64
ATTN_HEADS_PER_GROUP = 4
ATTN_GROUP_WIDTH = ATTN_HEADS_PER_GROUP * ATTN_HEAD_DIM
ATTN_GROUPS = ((128, 1), (512, 4), (2048, 16))
ATTN_BLOCK = 128
ATTN_STEP_BLOCKS = 8
DN_HEADS = 4
DN_HEAD_DIM = 128
DN_WIDTH = DN_HEADS * DN_HEAD_DIM
DN_CONV = 4
DN_CHUNK = 64
DN_BLOCK_CHUNKS = 8
DN_SOLVE_BLOCK = 16
FFN_CONV = 3

LANES = 128
SUBLANES_F32 = 8
SUBLANES_BF16 = 16
RELAYOUT_STRIDE = 4
VMEM_LIMIT_BYTES = 56 * 1024 * 1024

ROW_TILE = 512
FFN_COL_TILE = 6 * LANES


def _resident(shape):
    nd = len(shape)
    return pl.BlockSpec(shape, lambda *_: (0,) * nd, pipeline_mode=pl.Buffered(1))


def _layer_resident(stacked, layer):
    nd = stacked.ndim - 1
    return pl.BlockSpec((None,) + stacked.shape[1:], lambda *_: (layer,) + (0,) * nd, pipeline_mode=pl.Buffered(1))


def _params(*semantics):
    return pltpu.CompilerParams(dimension_semantics=semantics, vmem_limit_bytes=VMEM_LIMIT_BYTES)


def _rms_norm(x, w):
    return x * lax.rsqrt(jnp.mean(x * x, axis=-1, keepdims=True) + NORM_EPS) * w


def _silu(x):
    return x * (1.0 / (1.0 + jnp.exp(-x)))


def _sigmoid(x):
    return 1.0 / (1.0 + jnp.exp(-x))


def _stride_stages(d):
    if d > RELAYOUT_STRIDE and d % RELAYOUT_STRIDE == 0:
        return RELAYOUT_STRIDE, d // RELAYOUT_STRIDE
    return d, 1


def _to_dilated(y, relay_ref, o_ref, d):
    rows = y.shape[0]
    nslab = ATTN_GROUP_WIDTH // LANES
    d1, d2 = _stride_stages(d)
    for slab in range(nslab):
        relay_ref[0, slab] = y[:, slab * LANES:(slab + 1) * LANES]
    if d2 > 1:
        band = rows // d1
        for b in range(d1):
            for slab in range(nslab):
                relay_ref[1, slab, b * band:(b + 1) * band, :] = relay_ref[0, slab, pl.ds(b, band, stride=d1), :]
    for r in range(d):
        a, b = r // d1, r % d1
        for slab in range(nslab):
            c0 = r * ATTN_GROUP_WIDTH + slab * LANES
            if d2 > 1:
                piece = relay_ref[1, slab, pl.ds(b * (rows // d1) + a, rows // d, stride=d2), :]
            else:
                piece = relay_ref[0, slab, pl.ds(r, rows // d, stride=d), :]
            o_ref[:, c0:c0 + LANES] = piece.astype(o_ref.dtype)


def _from_dilated(x_ref, relay_ref, d):
    rows = relay_ref.shape[2]
    nslab = relay_ref.shape[1]
    width = nslab * LANES
    d1, d2 = _stride_stages(d)
    for r in range(d):
        a, b = r // d1, r % d1
        for slab in range(nslab):
            c0 = r * width + slab * LANES
            piece = x_ref[:, c0:c0 + LANES].astype(F32)
            if d2 > 1:
                relay_ref[1, slab, pl.ds(b * (rows // d1) + a, rows // d, stride=d2), :] = piece
            else:
                relay_ref[0, slab, pl.ds(r, rows // d, stride=d), :] = piece
    if d2 > 1:
        band = rows // d1
        for b in range(d1):
            for slab in range(nslab):
                relay_ref[0, slab, pl.ds(b, band, stride=d1), :] = relay_ref[1, slab, b * band:(b + 1) * band, :]
    return jnp.concatenate([relay_ref[0, slab] for slab in range(nslab)], axis=1)


def _inproj_body(x_ref, nw_ref, wmain_ref, wtail_ref, cw_ref, *refs, dilations, columns, dn_index, tiles_per_seq):
    n_out = len(dilations)
    out_refs = refs[:n_out]
    ydn_ref = refs[n_out]
    relay_refs = refs[n_out + 1:]
    w_refs = (wmain_ref, wtail_ref)
    i = pl.program_id(0)
    xn = _rms_norm(x_ref[...], nw_ref[...]).astype(BF16)

    halo = ydn_ref.shape[0] - x_ref.shape[0]
    rows = x_ref.shape[0]
    dn_ref = out_refs[dn_index]

    @pl.when(i % tiles_per_seq == 0)
    def _():
        ydn_ref[0:halo, :] = jnp.zeros((halo, ydn_ref.shape[1]), F32)

    @pl.when(i % tiles_per_seq > 0)
    def _():
        ydn_ref[0:halo, :] = ydn_ref[rows:rows + halo, :]

    step = 2 * DN_HEAD_DIM
    dn_w, dn_c0 = columns[dn_index]
    dn_pieces = list(range(0, 3 * DN_WIDTH, step))

    def dn_piece(cb):
        ydn_ref[halo:, cb:cb + step] = jnp.dot(xn, w_refs[dn_w][:, dn_c0 + cb:dn_c0 + cb + step],
                                               preferred_element_type=F32)
        for col in range(cb, cb + step, DN_HEAD_DIM):
            _dn_conv_block(ydn_ref, cw_ref, dn_ref, col, halo)

    others = [(j, p0) for j, (o_ref, d) in enumerate(zip(out_refs, dilations)) if j != dn_index
              for p0 in range(0, o_ref.shape[-1] // d, step)]
    every = max(1, len(others) // (len(dn_pieces) + 1))
    n_relay = 0
    for t, (j, p0) in enumerate(others):
        o_ref, d = out_refs[j], dilations[j]
        n = o_ref.shape[-1] // d
        pn = min(step, n - p0)
        wi, c0 = columns[j]
        y = jnp.dot(xn, w_refs[wi][:, c0 + p0:c0 + p0 + pn], preferred_element_type=F32)
        if d == 1:
            o_ref[:, p0:p0 + pn] = y.astype(o_ref.dtype)
        else:
            _to_dilated(y, relay_refs[n_relay], o_ref, d)
            n_relay += 1
        if dn_pieces and (t + 1) % every == 0:
            dn_piece(dn_pieces.pop(0))
    while dn_pieces:
        dn_piece(dn_pieces.pop(0))


def _dn_conv_block(ydn_ref, cw_ref, o_ref, col, halo):
    Dh = DN_HEAD_DIM
    rows = o_ref.shape[0]
    lanes = slice(col, col + Dh)
    acc = ydn_ref[halo:, lanes] * cw_ref[DN_CONV - 1:DN_CONV, lanes]
    for j in range(1, DN_CONV):
        acc = acc + ydn_ref[halo - j:halo - j + rows, lanes] * cw_ref[DN_CONV - 1 - j:DN_CONV - j, lanes]
    a = _silu(acc)
    if col < 2 * DN_WIDTH:
        a = a * lax.rsqrt(jnp.sum(a * a, axis=-1, keepdims=True) + NORM_EPS)
    if col < DN_WIDTH:
        a = a * (Dh ** -0.5)
    o_ref[:, lanes] = a.astype(o_ref.dtype)


def _in_projection(x2d, layer, norm_w, w_main, w_tail, dn_conv_w, out_widths, out_dtypes, dilations, columns,
                   dn_index, seq):
    T, D = x2d.shape
    grid = (T // ROW_TILE,)
    row = lambda i: (i, 0)
    n_relay = sum(1 for d in dilations if d > 1)
    body = functools.partial(_inproj_body, dilations=tuple(dilations), columns=tuple(columns), dn_index=dn_index,
                             tiles_per_seq=seq // ROW_TILE)
    return pl.pallas_call(
        body,
        grid=grid,
        in_specs=[pl.BlockSpec((ROW_TILE, D), row), _layer_resident(norm_w, layer), _layer_resident(w_main, layer),
                  _layer_resident(w_tail, layer), _layer_resident(dn_conv_w, layer)],
        out_specs=[pl.BlockSpec((ROW_TILE // d, d * n), row) for n, d in zip(out_widths, dilations)],
        out_shape=[jax.ShapeDtypeStruct((T // d, d * n), dt)
                   for n, dt, d in zip(out_widths, out_dtypes, dilations)],
        scratch_shapes=[pltpu.VMEM((SUBLANES_F32 + ROW_TILE, 3 * DN_WIDTH), F32)]
                       + [pltpu.VMEM((2, ATTN_GROUP_WIDTH // LANES, ROW_TILE, LANES), F32)] * n_relay,
        compiler_params=_params("arbitrary"),
        name="in_projection",
    )(x2d, norm_w, w_main, w_tail, dn_conv_w)


def _attn_body(*refs, spans, blocks_per_seq):
    ng = len(spans)
    blk = ATTN_BLOCK
    W = ATTN_GROUP_WIDTH
    H = ATTN_HEADS_PER_GROUP
    step = pl.program_id(1)
    row = lax.broadcasted_iota(jnp.int32, (blk, 2 * blk), 0)
    col = lax.broadcasted_iota(jnp.int32, (blk, 2 * blk), 1)
    rel = row + blk - col
    lane_head = lax.broadcasted_iota(jnp.int32, (blk, W), 1) // ATTN_HEAD_DIM
    lse_lane = lax.broadcasted_iota(jnp.int32, (blk, LANES), 1)

    units = []
    for g in range(ng):
        q_ref, kp_ref, kc_ref, vp_ref, vc_ref = refs[5 * g:5 * g + 5]
        o_ref, lse_ref = refs[5 * ng + 2 * g:5 * ng + 2 * g + 2]
        band = (rel >= 0) & (rel <= spans[g])
        first_block = step % blocks_per_seq[g] == 0
        band_first = band & (col >= jnp.where(first_block, blk, 0))
        for rr in range(q_ref.shape[1] // W):
            for jb in range(q_ref.shape[0] // blk):
                units.append(dict(q=q_ref, kp=kp_ref, kc=kc_ref, vp=vp_ref, vc=vc_ref, o=o_ref, lse=lse_ref,
                                  rr=rr, jb=jb, lanes=slice(rr * W, (rr + 1) * W),
                                  valid=band_first if jb == 0 else band))

    def window(prev_ref, cur_ref, u):
        jb, lanes = u["jb"], u["lanes"]
        if jb == 0:
            return jnp.concatenate([prev_ref[:, lanes], cur_ref[0:blk, lanes]], axis=0)
        return cur_ref[(jb - 1) * blk:(jb + 1) * blk, lanes]

    for u in units:
        jb = u["jb"]
        q = u["q"][jb * blk:(jb + 1) * blk, u["lanes"]] * (ATTN_HEAD_DIM ** -0.5)
        qs = jnp.concatenate([jnp.where(lane_head == h, q, jnp.zeros_like(q)) for h in range(H)], axis=0)
        u["scores"] = lax.dot_general(qs, window(u["kp"], u["kc"], u), (((1,), (1,)), ((), ())),
                                      preferred_element_type=F32)
    for u in units:
        probs, maxes = [], []
        for h in range(H):
            s = jnp.where(u["valid"], u["scores"][h * blk:(h + 1) * blk], -jnp.inf)
            m = jnp.max(s, axis=-1, keepdims=True)
            probs.append(jnp.exp((s - m).astype(BF16)))
            maxes.append(m)
        u["probs"] = jnp.concatenate(probs, axis=0)
        u["maxes"] = maxes
    ones = jnp.ones((2 * blk, LANES), BF16)
    for u in units:
        jb = u["jb"]
        pv = jnp.dot(u["probs"], window(u["vp"], u["vc"], u), preferred_element_type=F32)
        den = jnp.dot(u["probs"], ones, preferred_element_type=F32)
        o_acc = jnp.zeros((blk, W), F32)
        lse_acc = jnp.zeros((blk, LANES), F32)
        for h in range(H):
            den_h = den[h * blk:(h + 1) * blk]
            rden = 1.0 / den_h
            o_h = pv[h * blk:(h + 1) * blk] * jnp.concatenate([rden] * (W // LANES), axis=1)
            o_acc = jnp.where(lane_head == h, o_h, o_acc)
            lse_acc = jnp.where(lse_lane == h, u["maxes"][h] + jnp.log(den_h), lse_acc)
        u["o"][jb * blk:(jb + 1) * blk, u["lanes"]] = o_acc.astype(u["o"].dtype)
        u["lse"][jb * blk:(jb + 1) * blk, u["rr"] * LANES:(u["rr"] + 1) * LANES] = lse_acc


def _dilated_attention(qkv_groups, batch, seq):
    W = ATTN_GROUP_WIDTH
    operands, in_specs, out_specs, out_shape, spans, blocks_per_seq, steps = [], [], [], [], [], [], None
    for (window, d), (q, k, v) in zip(ATTN_GROUPS, qkv_groups):
        L = seq // d
        nq = min(ATTN_STEP_BLOCKS, L // ATTN_BLOCK)
        nr = min(ATTN_STEP_BLOCKS // nq, d)
        qrows = nq * ATTN_BLOCK
        n_i = L // qrows
        assert L % qrows == 0 and d % nr == 0
        assert steps in (None, (d // nr) * n_i)
        steps = (d // nr) * n_i
        view = lambda t, L=L: t.reshape(batch, L, t.shape[-1])
        cur = lambda b, s, n_i=n_i: (b, s % n_i, s // n_i)
        prev = lambda b, s, n_i=n_i, nq=nq: (b, jnp.maximum((s % n_i) * nq - 1, 0), s // n_i)
        cur_spec = pl.BlockSpec((None, qrows, nr * W), cur)
        prev_spec = pl.BlockSpec((None, ATTN_BLOCK, nr * W), prev)
        operands += [view(q), view(k), view(k), view(v), view(v)]
        in_specs += [cur_spec, prev_spec, cur_spec, prev_spec, cur_spec]
        out_specs += [cur_spec, pl.BlockSpec((None, qrows, nr * LANES), cur)]
        out_shape += [jax.ShapeDtypeStruct((batch, L, d * W), BF16), jax.ShapeDtypeStruct((batch, L, d * LANES), F32)]
        spans.append(window // d)
        blocks_per_seq.append(n_i)
    outs = pl.pallas_call(
        functools.partial(_attn_body, spans=tuple(spans), blocks_per_seq=tuple(blocks_per_seq)),
        grid=(batch, steps),
        in_specs=in_specs,
        out_specs=out_specs,
        out_shape=out_shape,
        compiler_params=_params("parallel", "parallel"),
        name="dilated_attention",
    )(*operands)
    o_groups = [o.reshape(-1, o.shape[-1]) for o in outs[0::2]]
    lse_groups = [l.reshape(-1, l.shape[-1]) for l in outs[1::2]]
    return o_groups, lse_groups


def _mm(a, b):
    return jnp.dot(a, b, preferred_element_type=F32)


def _mm_nt(a, b):
    return lax.dot_general(a, b, (((1,), (1,)), ((), ())), preferred_element_type=F32)


def _mm_tn(a, b):
    return lax.dot_general(a, b, (((0,), (0,)), ((), ())), preferred_element_type=F32)


def _deltanet_body(x_ref, sm_ref, z_ref, avec_ref, dtb_ref, onw_ref, o_ref, state_ref):
    C = DN_CHUNK
    Dh = DN_HEAD_DIM
    rows = x_ref.shape[0]
    blk = pl.program_id(1)

    @pl.when(blk == 0)
    def _():
        state_ref[...] = jnp.zeros_like(state_ref)

    sm = sm_ref[...]
    beta_all = _sigmoid(sm)
    sp_in = sm + dtb_ref[...]
    softplus = jnp.maximum(sp_in, 0.0) + jnp.log(1.0 + jnp.exp(-jnp.abs(sp_in)))
    g_all = avec_ref[...] * softplus
    row_in_chunk = lax.broadcasted_iota(jnp.int32, (rows, LANES), 0) % C
    gc_all = g_all
    s = 1
    while s < C:
        gc_all = gc_all + jnp.where(row_in_chunk >= s, pltpu.roll(gc_all, s, axis=0), 0.0)
        s *= 2
    sel = jnp.where(lax.broadcasted_iota(jnp.int32, (SUBLANES_F32, LANES), 1)
                    == lax.broadcasted_iota(jnp.int32, (SUBLANES_F32, LANES), 0) + DN_HEADS, 1.0, 0.0).astype(F32)

    ri = lax.broadcasted_iota(jnp.int32, (C, C), 0)
    ci = lax.broadcasted_iota(jnp.int32, (C, C), 1)
    incl = ri >= ci
    strict = ri > ci
    same_block = (ri // DN_SOLVE_BLOCK) == (ci // DN_SOLVE_BLOCK)
    nc = rows // C

    items = []
    for c in range(nc):
        rs = slice(c * C, (c + 1) * C)
        gc_c = gc_all[rs]
        gc_rows = lax.dot_general(sel, gc_c, (((1,), (1,)), ((), ())), preferred_element_type=F32,
                                  precision=HIGHEST)
        for h in range(DN_HEADS):
            q_b = x_ref[rs, h * Dh:(h + 1) * Dh]
            k_b = x_ref[rs, (DN_HEADS + h) * Dh:(DN_HEADS + h + 1) * Dh]
            k = k_b.astype(F32)
            v = x_ref[rs, (2 * DN_HEADS + h) * Dh:(2 * DN_HEADS + h + 1) * Dh].astype(F32)
            beta = beta_all[rs, h:h + 1]
            gcol = gc_c[:, DN_HEADS + h:DN_HEADS + h + 1]
            grow = gc_rows[h:h + 1, :]
            decay = jnp.exp(jnp.where(incl, gcol - grow, -jnp.inf))
            kb = k * beta
            eg = jnp.exp(gcol)
            g_last = gcol[C - 1:C, :]
            qk_kk = _mm_nt(jnp.concatenate([q_b, kb.astype(BF16)], axis=0), k_b)
            m = jnp.where(strict, qk_kk[C:] * decay, 0.0)
            md = jnp.where(same_block, m, 0.0)
            items.append(dict(
                rs=rs, h=h, qk=(qk_kk[:C] * decay).astype(BF16), md=md, mo=m - md,
                rhs=jnp.concatenate([v * beta, kb * eg], axis=1),
                qe=q_b.astype(F32) * eg, k_dec=(k * jnp.exp(g_last - gcol)).astype(BF16),
                g_last=jnp.exp(g_last)))

    for it in items:
        md_b = it["md"].astype(BF16)
        it["p"] = -it["md"]
        it["mk"] = _mm(md_b, md_b)
    for step in range(3):
        for it in items:
            mk_b = it["mk"].astype(BF16)
            p_b = it["p"].astype(BF16)
            if step < 2:
                both = _mm(jnp.concatenate([mk_b, p_b], axis=0), mk_b)
                it["p"] = it["p"] + it["mk"] + both[C:]
                it["mk"] = both[:C]
            else:
                it["p"] = it["p"] + it["mk"] + _mm(p_b, mk_b)
    for it in items:
        it["p_b"] = it["p"].astype(BF16)
        it["n"] = it["mo"] + _mm(it["p_b"], it["mo"].astype(BF16))
    for it in items:
        it["n_b"] = it["n"].astype(BF16)
        it["n2"] = _mm(it["n_b"], it["n_b"])
    for it in items:
        it["q"] = it["n2"] - it["n"] - _mm(it["n2"].astype(BF16), it["n_b"])
    for it in items:
        it["tinv"] = it["q"] + it["p"] + _mm(it["q"].astype(BF16), it["p_b"])
    for it in items:
        uw = it["rhs"] + _mm(it["tinv"].astype(BF16), it["rhs"].astype(BF16))
        it["uw_b"] = uw.astype(BF16)
    for it in items:
        ktuw = _mm_tn(it["k_dec"], it["uw_b"])
        it["ktu"] = ktuw[:, :Dh]
        it["ktw_b"] = ktuw[:, Dh:].astype(BF16)
    for it in items:
        quw = _mm(it["qk"], it["uw_b"])
        it["o0"] = quw[:, :Dh]
        it["q_eff_b"] = (it["qe"] - quw[:, Dh:]).astype(BF16)

    states = [state_ref[h] for h in range(DN_HEADS)]
    for c in range(nc):
        chunk_items = items[c * DN_HEADS:(c + 1) * DN_HEADS]
        prods = [_mm(jnp.concatenate([it["ktw_b"], it["q_eff_b"]], axis=0), states[it["h"]].astype(BF16))
                 for it in chunk_items]
        for it, prod in zip(chunk_items, prods):
            h, rs = it["h"], it["rs"]
            o = prod[Dh:] + it["o0"]
            states[h] = states[h] * it["g_last"] + it["ktu"] - prod[:Dh]
            zg = z_ref[rs, h * Dh:(h + 1) * Dh].astype(F32)
            o_ref[rs, h * Dh:(h + 1) * Dh] = (_rms_norm(o, onw_ref[...]) * _silu(zg)).astype(o_ref.dtype)
    for h in range(DN_HEADS):
        state_ref[h] = states[h]


def _deltanet(dn_qkv, small, z, a_log, dt_bias, onorm_w, batch, seq):
    T = dn_qkv.shape[0]
    rows = DN_BLOCK_CHUNKS * DN_CHUNK
    nblocks = seq // rows
    lane = jnp.arange(LANES)
    in_decay_lanes = (lane >= DN_HEADS) & (lane < 2 * DN_HEADS)
    idx = jnp.clip(lane - DN_HEADS, 0, DN_HEADS - 1)
    avec = jnp.where(in_decay_lanes, -jnp.exp(a_log.astype(F32))[idx], 0.0).reshape(1, LANES)
    dtb = jnp.where(in_decay_lanes, dt_bias.astype(F32)[idx], 0.0).reshape(1, LANES)
    row = lambda b, c: (b * nblocks + c, 0)
    return pl.pallas_call(
        _deltanet_body,
        grid=(batch, nblocks),
        in_specs=[pl.BlockSpec((rows, 3 * DN_WIDTH), row),
                  pl.BlockSpec((rows, LANES), row),
                  pl.BlockSpec((rows, DN_WIDTH), row),
                  _resident((1, LANES)), _resident((1, LANES)), _resident((1, DN_HEAD_DIM))],
        out_specs=pl.BlockSpec((rows, DN_WIDTH), row),
        out_shape=jax.ShapeDtypeStruct((T, DN_WIDTH), BF16),
        scratch_shapes=[pltpu.VMEM((DN_HEADS, DN_HEAD_DIM, DN_HEAD_DIM), F32)],
        compiler_params=_params("parallel", "arbitrary"),
        name="gated_deltanet",
    )(dn_qkv, small, z, avec, dtb, onorm_w.astype(F32).reshape(1, DN_HEAD_DIM))


def _mixer_output(o_refs, lse_refs, relay_refs, dilations, ob_ref, gates_ref, x_ref, expand_ref, wpa_ref, wpb_ref,
                  wo_ref):
    D = x_ref.shape[-1]
    relay_refs = list(relay_refs)
    o_nat, lses = [], []
    for o_ref, lse_ref, d in zip(o_refs, lse_refs, dilations):
        if d == 1:
            o_nat.append(o_ref[...].astype(F32))
            lses.append(lse_ref[...])
        else:
            o_nat.append(_from_dilated(o_ref, relay_refs.pop(0), d))
            lses.append(_from_dilated(lse_ref, relay_refs.pop(0), d))
    mx = functools.reduce(jnp.maximum, lses)
    es = [jnp.exp(l - mx) for l in lses]
    inv = 1.0 / functools.reduce(lambda a, b: a + b, es)
    o_a = None
    for e, o_g in zip(es, o_nat):
        alpha = e * inv
        hi = alpha.astype(BF16)
        lo = (alpha - hi.astype(F32)).astype(BF16)
        alpha_wide = jnp.dot(jnp.concatenate([hi, lo], axis=1), expand_ref[...], preferred_element_type=F32)
        term = alpha_wide * o_g
        o_a = term if o_a is None else o_a + term
    pa = jnp.dot(o_a.astype(BF16), wpa_ref[...], preferred_element_type=F32)
    pb = jnp.dot(ob_ref[...], wpb_ref[...], preferred_element_type=F32)
    ga = gates_ref[:, :D].astype(F32)
    gb = gates_ref[:, D:].astype(F32)
    y = _sigmoid(ga) * pa + _sigmoid(gb) * pb
    return x_ref[...] + jnp.dot(y.astype(BF16), wo_ref[...], preferred_element_type=F32)


def _mixer_ffn_body(*refs, dilations, tiles_per_seq, col_tile, final_norm):
    ng = len(dilations)
    o_refs = refs[:ng]
    lse_refs = refs[ng:2 * ng]
    (ob_ref, gates_ref, x_ref, expand_ref, wpa_ref, wpb_ref, wo_ref,
     nw_ref, wup_ref, cw_ref, cb_ref, wdown_ref, fw_ref, out_ref, act_ref, ucarry_ref) = refs[2 * ng:2 * ng + 16]
    relay_refs = refs[2 * ng + 16:]
    dff = wdown_ref.shape[0]
    halo_rows = ucarry_ref.shape[0]
    rows = x_ref.shape[0]
    i = pl.program_id(0)

    @pl.when(i % tiles_per_seq == 0)
    def _():
        ucarry_ref[...] = jnp.zeros_like(ucarry_ref)

    h = _mixer_output(o_refs, lse_refs, relay_refs, dilations, ob_ref, gates_ref, x_ref, expand_ref, wpa_ref,
                      wpb_ref, wo_ref)
    hn = _rms_norm(h, nw_ref[...]).astype(BF16)

    def conv(c0, ct):
        u = jnp.dot(hn, wup_ref[:, c0:c0 + ct], preferred_element_type=F32)
        ue = jnp.concatenate([ucarry_ref[:, c0:c0 + ct], u], axis=0)
        ucarry_ref[:, c0:c0 + ct] = u[rows - halo_rows:]
        acc = ue * cw_ref[FFN_CONV - 1:FFN_CONV, c0:c0 + ct]
        for j in range(1, FFN_CONV):
            acc = acc + pltpu.roll(ue, j, axis=0) * cw_ref[FFN_CONV - 1 - j:FFN_CONV - j, c0:c0 + ct]
        return acc[halo_rows:] + cb_ref[:, c0:c0 + ct]

    for c0 in range(0, dff, col_tile):
        ct = min(col_tile, dff - c0)
        act_ref[:, c0:c0 + ct] = (_silu(conv(c0, ct)) * conv(dff + c0, ct)).astype(BF16)
    y = h + jnp.dot(act_ref[...], wdown_ref[...], preferred_element_type=F32)
    if final_norm:
        y = _rms_norm(y, fw_ref[...])
    out_ref[...] = y


def _mixer_ffn(o_groups, lse_groups, dilations, o_b, gates, x2d, layer, w_pa, w_pb, w_o, seq, norm_w, w_up, conv_w,
               conv_b, w_down, final_w, final_norm):
    T, D = x2d.shape
    W = ATTN_GROUP_WIDTH
    dff = w_down.shape[1]
    col_tile = FFN_COL_TILE
    lane = jnp.arange(2 * LANES)[:, None] % LANES
    expand = (lane == (jnp.arange(W)[None, :] // ATTN_HEAD_DIM)).astype(BF16)
    row = lambda i: (i, 0)
    tile = lambda n, d=1: pl.BlockSpec((ROW_TILE // d, d * n), row)
    relay = []
    for d in dilations:
        if d > 1:
            relay += [pltpu.VMEM((2, W // LANES, ROW_TILE, LANES), F32), pltpu.VMEM((2, 1, ROW_TILE, LANES), F32)]
    body = functools.partial(_mixer_ffn_body, dilations=tuple(dilations), tiles_per_seq=seq // ROW_TILE,
                             col_tile=col_tile, final_norm=final_norm)
    stacks = (w_pa, w_pb, w_o, norm_w, w_up, conv_w, conv_b, w_down)
    return pl.pallas_call(
        body,
        grid=(T // ROW_TILE,),
        in_specs=[tile(W, d) for d in dilations] + [tile(LANES, d) for d in dilations]
                 + [tile(DN_WIDTH), tile(2 * D), tile(D), _resident(expand.shape)]
                 + [_layer_resident(w, layer) for w in stacks] + [_resident((1, D))],
        out_specs=tile(D),
        out_shape=jax.ShapeDtypeStruct((T, D), F32),
        scratch_shapes=[pltpu.VMEM((ROW_TILE, dff), BF16), pltpu.VMEM((SUBLANES_F32, 2 * dff), F32)] + relay,
        compiler_params=_params("arbitrary"),
        name="mixer_out_conv_glu_ffn",
    )(*o_groups, *lse_groups, o_b, gates, x2d, expand, *stacks, final_w.reshape(1, D))


def _split_in_weights(w_in):
    ng = len(ATTN_GROUPS)
    W = ATTN_GROUP_WIDTH
    aw = ng * W
    n_main = 3 * aw + 3 * DN_WIDTH
    n_small = 2 * DN_HEADS
    D = w_in.shape[1]
    main = w_in.astype(BF16)
    small = jnp.pad(w_in[:, :, n_main:n_main + n_small], ((0, 0), (0, 0), (0, LANES - n_small)))
    tail = jnp.concatenate([w_in[:, :, n_main + n_small:], small], axis=2).astype(BF16)
    columns = [(0, part * aw + g * W) for g in range(ng) for part in range(3)]
    columns += [(0, 3 * aw), (1, 0), (1, DN_WIDTH), (1, DN_WIDTH + 2 * D)]
    widths = [W] * (3 * ng) + [3 * DN_WIDTH, DN_WIDTH, 2 * D, LANES]
    dtypes = [BF16] * (3 * ng + 3) + [F32]
    return main, tail, columns, widths, dtypes


def kernel(x, norm1_w, w_in, dn_conv_w, dn_a_log, dn_dt_bias, dn_onorm_w, w_pa, w_pb, w_o, norm2_w, w_up,
           ffn_conv_w, ffn_conv_b, w_down, final_norm_w):
    B, S, D = x.shape
    depth = w_in.shape[0]
    assert S % ROW_TILE == 0 and S % (DN_BLOCK_CHUNKS * DN_CHUNK) == 0
    assert all(ROW_TILE % (SUBLANES_BF16 * d) == 0 for _, d in ATTN_GROUPS)
    xf = x.astype(F32).reshape(B * S, D)
    w_main, w_tail, columns, widths, dtypes = _split_in_weights(w_in)
    row_stack = lambda p: p.astype(F32).reshape(depth, 1, -1)
    norm1_s, norm2_s, conv_b_s = row_stack(norm1_w), row_stack(norm2_w), row_stack(ffn_conv_b)
    dn_conv_s, ffn_conv_s = dn_conv_w.astype(F32), ffn_conv_w.astype(F32)
    w_pa_s, w_pb_s, w_o_s, w_up_s, w_down_s = (w.astype(BF16) for w in (w_pa, w_pb, w_o, w_up, w_down))
    ng = len(ATTN_GROUPS)
    group_dilations = [d for _, d in ATTN_GROUPS]
    out_dilations = [d for d in group_dilations for _ in range(3)] + [1] * (len(widths) - 3 * ng)
    for l in range(depth):
        outs = _in_projection(xf, l, norm1_s, w_main, w_tail, dn_conv_s, widths, dtypes, out_dilations, columns,
                              dn_index=3 * ng, seq=S)
        o_groups, lse_groups = _dilated_attention([outs[3 * g:3 * g + 3] for g in range(ng)], B, S)
        dn_qkv, z, gates, small = outs[3 * ng:]
        o_b = _deltanet(dn_qkv, small, z, dn_a_log[l], dn_dt_bias[l], dn_onorm_w[l], B, S)
        xf = _mixer_ffn(o_groups, lse_groups, group_dilations, o_b, gates, xf, l, w_pa_s, w_pb_s, w_o_s, S, norm2_s,
                        w_up_s, ffn_conv_s, conv_b_s, w_down_s, final_norm_w.astype(F32),
                        final_norm=(l == depth - 1))
    return xf.reshape(B, S, D).astype(x.dtype)
```
